```python
import math
import jax, jax.numpy as jnp
from jax import lax
import numpy as np

D_MODEL = 1024
BATCH = 4
SEQ = 4096
DEPTH = 1
DEC_BATCH = 32
DEC_SEQ = 8
PAST_LEN = 16384
PAGE_SIZE = 128

FOX_HEADS = 8
FOX_HEAD_DIM = 64
FOX_WIDTH = FOX_HEADS * FOX_HEAD_DIM
FOX_SCALE = FOX_HEAD_DIM ** -0.5
Q_BLOCK = 128
GLA_HEADS = 4
GLA_DK = D_MODEL // 2 // GLA_HEADS
GLA_DV = D_MODEL // GLA_HEADS
GLA_KW = GLA_HEADS * GLA_DK
GLA_VW = GLA_HEADS * GLA_DV
GLA_RANK = 16
GLA_TAU = 16.0
GLA_CHUNK = 64
EPS = 1e-6

SPLIT_SIZES = (FOX_WIDTH, FOX_WIDTH, FOX_WIDTH, FOX_HEADS, FOX_WIDTH,
               GLA_KW, GLA_KW, GLA_VW, GLA_RANK, GLA_VW,
               D_MODEL, D_MODEL)
SPLIT_POINTS = tuple(int(v) for v in np.cumsum(SPLIT_SIZES)[:-1])
D_IN = int(sum(SPLIT_SIZES))

kernel_name = "fox_gla_parallel_gated_decoder_step"


def _rmsnorm(x, g):
    xf = x.astype(jnp.float32)
    y = xf * lax.rsqrt(jnp.mean(xf * xf, axis=-1, keepdims=True) + EPS)
    return (y * g.astype(jnp.float32)).astype(x.dtype)


def _project(x, ln_g, w_in, fox_b_f, q_norm_g, k_norm_g, gla_w_a2, gla_b_a):
    B, L, _ = x.shape
    h = _rmsnorm(x, ln_g)
    z = h @ w_in
    fq, fk, fv, ff, fg, gq, gk, gv, glr, gg, ma, mb = jnp.split(z, SPLIT_POINTS, axis=-1)
    fq = _rmsnorm(fq.reshape(B, L, FOX_HEADS, FOX_HEAD_DIM), q_norm_g)
    fk = _rmsnorm(fk.reshape(B, L, FOX_HEADS, FOX_HEAD_DIM), k_norm_g)
    fv = fv.reshape(B, L, FOX_HEADS, FOX_HEAD_DIM)
    logf = jax.nn.log_sigmoid((ff + fox_b_f).astype(jnp.float32))
    gq = gq.reshape(B, L, GLA_HEADS, GLA_DK) * (GLA_DK ** -0.5)
    gk = gk.reshape(B, L, GLA_HEADS, GLA_DK)
    gv = gv.reshape(B, L, GLA_HEADS, GLA_DV)
    log_a = jax.nn.log_sigmoid((glr @ gla_w_a2 + gla_b_a).astype(jnp.float32)) / GLA_TAU
    log_a = log_a.reshape(B, L, GLA_HEADS, GLA_DK)
    return fq, fk, fv, logf, fg, gq, gk, gv, log_a, gg, ma, mb


def _fox_prompt(q, k, v, logf):
    B, L, H, hd = q.shape
    nb = L // Q_BLOCK
    c = jnp.cumsum(logf, axis=1).transpose(0, 2, 1)
    q_blocks = q.reshape(B, nb, Q_BLOCK, H, hd).transpose(1, 0, 2, 3, 4)
    key_pos = jnp.arange(L)

    def block(args):
        i, q_i = args
        q_pos = i * Q_BLOCK + jnp.arange(Q_BLOCK)
        c_q = lax.dynamic_slice_in_dim(c, i * Q_BLOCK, Q_BLOCK, axis=2)
        s = jnp.einsum('bqhd,bkhd->bhqk', q_i, k).astype(jnp.float32) * FOX_SCALE
        s = s + c_q[..., None] - c[:, :, None, :]
        s = jnp.where(key_pos[None, :] <= q_pos[:, None], s, -jnp.inf)
        p = jax.nn.softmax(s, axis=-1)
        return jnp.einsum('bhqk,bkhd->bqhd', p.astype(v.dtype), v)

    o = lax.map(block, (jnp.arange(nb), q_blocks))
    return o.transpose(1, 0, 2, 3, 4).reshape(B, L, H, hd)


def _fox_sample(q, k_new, v_new, logf_new, cache_k, cache_v, cache_logf, page_table):
    Bd, T, H, hd = q.shape
    n_pages = page_table.shape[1]
    P = n_pages * PAGE_SIZE
    k_past = cache_k[page_table].reshape(Bd, P, H, hd)
    v_past = cache_v[page_table].reshape(Bd, P, H, hd)
    lf_past = cache_logf[page_table].reshape(Bd, P, H).astype(jnp.float32)
    c_past = jnp.cumsum(lf_past, axis=1)
    c_new = c_past[:, -1:, :] + jnp.cumsum(logf_new.astype(jnp.float32), axis=1)
    c_q = c_new.transpose(0, 2, 1)[..., None]
    s_past = jnp.einsum('bqhd,bkhd->bhqk', q, k_past).astype(jnp.float32) * FOX_SCALE
    s_past = s_past + c_q - c_past.transpose(0, 2, 1)[:, :, None, :]
    s_new = jnp.einsum('bqhd,bkhd->bhqk', q, k_new).astype(jnp.float32) * FOX_SCALE
    s_new = s_new + c_q - c_new.transpose(0, 2, 1)[:, :, None, :]
    s_new = jnp.where(jnp.tril(jnp.ones((T, T), dtype=bool)), s_new, -jnp.inf)
    p = jax.nn.softmax(jnp.concatenate([s_past, s_new], axis=-1), axis=-1)
    o = (jnp.einsum('bhqk,bkhd->bqhd', p[..., :P].astype(v_past.dtype), v_past)
         + jnp.einsum('bhqk,bkhd->bqhd', p[..., P:].astype(v_new.dtype), v_new))
    return o


def _gla(q, k, v, log_a, s0):
    B, L, Hh, dk = q.shape
    dv = v.shape[-1]
    C = math.gcd(L, GLA_CHUNK)
    n = L // C

    def chunks(t):
        return t.astype(jnp.float32).reshape(B, n, C, Hh, t.shape[-1]).transpose(1, 0, 3, 2, 4)

    causal = jnp.tril(jnp.ones((C, C), dtype=bool))

    def step(S, inp):
        qc, kc, vc, gc = inp
        b = jnp.cumsum(gc, axis=2)
        b_last = b[:, :, -1:, :]
        q_in = qc * jnp.exp(b)
        k_in = kc * jnp.exp(-b)
        att = jnp.where(causal, jnp.einsum('bhtd,bhsd->bhts', q_in, k_in), 0.0)
        o = jnp.einsum('bhts,bhsv->bhtv', att, vc) + jnp.einsum('bhtd,bhdv->bhtv', q_in, S)
        S = (jnp.exp(b_last[:, :, 0, :])[..., None] * S
             + jnp.einsum('bhsd,bhsv->bhdv', kc * jnp.exp(b_last - b), vc))
        return S, o

    S, o = lax.scan(step, s0.astype(jnp.float32), (chunks(q), chunks(k), chunks(v), chunks(log_a)))
    o = o.transpose(1, 0, 3, 2, 4).reshape(B, L, Hh, dv)
    return o.astype(v.dtype), S


def _merge(x, o_fox, fg, o_gla, gg, ma, mb, gla_norm_g, w_up_a, w_up_b, w_out):
    B, L, _ = x.shape
    a = (o_fox.reshape(B, L, FOX_WIDTH) * jax.nn.silu(fg)) @ w_up_a
    o_gla = _rmsnorm(o_gla, gla_norm_g).reshape(B, L, GLA_VW)
    b = (o_gla * jax.nn.silu(gg)) @ w_up_b
    m = jax.nn.sigmoid(ma) * a + jax.nn.sigmoid(mb) * b
    return x + m @ w_out


def setup_inputs(seed: int = 0) -> dict:
    key = jax.random.key(seed)
    ks = jax.random.split(key, 20)
    n_pages = PAST_LEN // PAGE_SIZE
    n_used = DEC_BATCH * n_pages
    n_phys = n_used + n_used // 4
    f32 = jnp.float32
    nrm = jax.random.normal
    x_prompt = nrm(ks[0], (BATCH, SEQ, D_MODEL), f32)
    x_sample = nrm(ks[1], (DEC_BATCH, DEC_SEQ, D_MODEL), f32)
    cache_k = nrm(ks[2], (DEPTH, n_phys, PAGE_SIZE, FOX_HEADS, FOX_HEAD_DIM), f32)
    cache_v = nrm(ks[3], (DEPTH, n_phys, PAGE_SIZE, FOX_HEADS, FOX_HEAD_DIM), f32)
    cache_logf = jax.nn.log_sigmoid(5.0 + 1.5 * nrm(ks[4], (DEPTH, n_phys, PAGE_SIZE, FOX_HEADS), f32))
    state_gla = 0.5 * nrm(ks[5], (DEPTH, DEC_BATCH, GLA_HEADS, GLA_DK, GLA_DV), f32)
    page_table = jax.random.permutation(ks[6], n_phys)[:n_used].reshape(DEC_BATCH, n_pages).astype(jnp.int32)
    ln_g = 1.0 + 0.05 * nrm(ks[7], (DEPTH, D_MODEL), f32)
    w_in = nrm(ks[8], (DEPTH, D_MODEL, D_IN), f32) * D_MODEL ** -0.5
    fox_b_f = 3.0 + 3.0 * jax.random.uniform(ks[9], (DEPTH, FOX_HEADS), f32)
    q_norm_g = 1.0 + 0.05 * nrm(ks[10], (DEPTH, FOX_HEAD_DIM), f32)
    k_norm_g = 1.0 + 0.05 * nrm(ks[11], (DEPTH, FOX_HEAD_DIM), f32)
    gla_w_a2 = nrm(ks[12], (DEPTH, GLA_RANK, GLA_KW), f32) * GLA_RANK ** -0.5
    gla_b_a = 0.1 * nrm(ks[13], (DEPTH, GLA_KW), f32)
    gla_norm_g = 1.0 + 0.05 * nrm(ks[14], (DEPTH, GLA_DV), f32)
    w_up_a = nrm(ks[15], (DEPTH, FOX_WIDTH, D_MODEL), f32) * FOX_WIDTH ** -0.5
    w_up_b = nrm(ks[16], (DEPTH, GLA_VW, D_MODEL), f32) * GLA_VW ** -0.5
    w_out = nrm(ks[17], (DEPTH, D_MODEL, D_MODEL), f32) * D_MODEL ** -0.5
    return {"x_prompt": x_prompt, "x_sample": x_sample, "cache_k": cache_k, "cache_v": cache_v,
            "cache_logf": cache_logf, "state_gla": state_gla, "page_table": page_table,
            "ln_g": ln_g, "w_in": w_in, "fox_b_f": fox_b_f, "q_norm_g": q_norm_g, "k_norm_g": k_norm_g,
            "gla_w_a2": gla_w_a2, "gla_b_a": gla_b_a, "gla_norm_g": gla_norm_g,
            "w_up_a": w_up_a, "w_up_b": w_up_b, "w_out": w_out}


def reference(x_prompt, x_sample, cache_k, cache_v, cache_logf, state_gla, page_table,
              ln_g, w_in, fox_b_f, q_norm_g, k_norm_g, gla_w_a2, gla_b_a, gla_norm_g,
              w_up_a, w_up_b, w_out):
    yp, ys = x_prompt, x_sample
    kp, vp, lfp, sp = [], [], [], []
    kq, vq, lfq, sq = [], [], [], []
    for l in range(DEPTH):
        proj = (ln_g[l], w_in[l], fox_b_f[l], q_norm_g[l], k_norm_g[l], gla_w_a2[l], gla_b_a[l])
        outp = (gla_norm_g[l], w_up_a[l], w_up_b[l], w_out[l])
        fq, fk, fv, logf, fg, gq, gk, gv, la, gg, ma, mb = _project(yp, *proj)
        o_f = _fox_prompt(fq, fk, fv, logf)
        s0 = jnp.zeros((yp.shape[0], GLA_HEADS, GLA_DK, GLA_DV), jnp.float32)
        o_g, s_end = _gla(gq, gk, gv, la, s0)
        kp.append(fk); vp.append(fv); lfp.append(logf); sp.append(s_end)
        yp = _merge(yp, o_f, fg, o_g, gg, ma, mb, *outp)
        fq, fk, fv, logf, fg, gq, gk, gv, la, gg, ma, mb = _project(ys, *proj)
        o_f = _fox_sample(fq, fk, fv, logf, cache_k[l], cache_v[l], cache_logf[l], page_table)
        o_g, s_end = _gla(gq, gk, gv, la, state_gla[l])
        kq.append(fk); vq.append(fv); lfq.append(logf); sq.append(s_end)
        ys = _merge(ys, o_f, fg, o_g, gg, ma, mb, *outp)
    k_prompt = jnp.stack(kp)
    v_prompt = jnp.stack(vp)
    logf_prompt = jnp.stack(lfp)
    gla_prompt = jnp.stack(sp)
    k_sample = jnp.stack(kq)
    v_sample = jnp.stack(vq)
    logf_sample = jnp.stack(lfq)
    gla_sample = jnp.stack(sq)
    return (yp, ys, k_prompt, v_prompt, logf_prompt, gla_prompt, k_sample, v_sample, logf_sample, gla_sample)
```

```python
import functools

import jax
import jax.numpy as jnp
from jax import lax
from jax.experimental import pallas as pl
from jax.experimental.pallas import tpu as pltpu

F32 = jnp.float32
BF16 = jnp.bfloat16

D_MODEL = 1024
FOX_HEADS = 8
FOX_HEAD_DIM = 64
FOX_WIDTH = FOX_HEADS * FOX_HEAD_DIM
FOX_SCALE = FOX_HEAD_DIM ** -0.5
GLA_HEADS = 4
GLA_DK = 128
GLA_DV = 256
GLA_KW = GLA_HEADS * GLA_DK
GLA_VW = GLA_HEADS * GLA_DV
GLA_RANK = 16
GLA_TAU = 16.0
EPS = 1e-6
PAGE_SIZE = 128

LANES = 128
SUBLANES = 8
VMEM_LIMIT = 48 * 1024 * 1024

_SIZES = (FOX_WIDTH, FOX_WIDTH, FOX_WIDTH, FOX_HEADS, FOX_WIDTH,
          GLA_KW, GLA_KW, GLA_VW, GLA_RANK, GLA_VW, D_MODEL, D_MODEL)
_OFF = [0]
for _s in _SIZES:
    _OFF.append(_OFF[-1] + _s)
(_FQ, _FK, _FV, _FF, _FG, _GQ, _GK, _GV, _GLR, _GG, _MA, _MB) = _OFF[:-1]

_WF_SMALL = 4 * FOX_WIDTH
_WF_ROWS = _WF_SMALL + 32
_WT_COLS = FOX_WIDTH + GLA_KW + GLA_VW + GLA_VW + 2 * D_MODEL


def _dot_nt(a, b):
    return lax.dot_general(a, b, (((1,), (1,)), ((), ())), preferred_element_type=F32)


def _dot(a, b):
    return jnp.dot(a, b, preferred_element_type=F32)


def _log_sigmoid(x):
    return -(jnp.maximum(-x, 0.0) + jnp.log1p(jnp.exp(-jnp.abs(x))))


def _split3(a):
    hi = a.astype(BF16).astype(F32)
    r = a - hi
    mid = r.astype(BF16).astype(F32)
    lo = (r - mid).astype(BF16).astype(F32)
    return hi, mid, lo


def _lane_tile(a, n):
    return a if n == 1 else jnp.concatenate([a] * n, axis=1)


def _proj_kernel(x_ref, lng_ref, wf_ref, wt_ref, qg_ref, kg_ref, fb_ref, wa2t_ref, bat_ref,
                 qT_ref, kT_ref, ktok_ref, cp_ref, vT_ref, vTb_ref, lfT_ref, fgs_ref,
                 gq_ref, gkT_ref, gv_ref, la_ref, laT_ref, ggs_ref, sa_ref, sb_ref,
                 carry_ref, *, tm):
    nrep = tm // LANES

    @pl.when(pl.program_id(1) == 0)
    def _():
        carry_ref[...] = jnp.zeros_like(carry_ref)

    x = x_ref[0]
    ms = jnp.mean(x * x, axis=-1, keepdims=True)
    h = ((x * lax.rsqrt(ms + EPS)) * lng_ref[...]).astype(BF16)

    def headnorm(t, g_ref):
        outs = []
        for hh in range(FOX_HEADS):
            blk = t[hh * FOX_HEAD_DIM:(hh + 1) * FOX_HEAD_DIM]
            ssq = jnp.sum(blk * blk, axis=0, keepdims=True) * (1.0 / FOX_HEAD_DIM)
            g = _lane_tile(g_ref[hh * FOX_HEAD_DIM:(hh + 1) * FOX_HEAD_DIM], nrep)
            outs.append((blk * lax.rsqrt(ssq + EPS)) * g)
        return jnp.concatenate(outs, axis=0)

    q = headnorm(_dot_nt(wf_ref[0:FOX_WIDTH], h), qg_ref)
    qT_ref[0] = (q * FOX_SCALE).astype(BF16)
    k = headnorm(_dot_nt(wf_ref[FOX_WIDTH:2 * FOX_WIDTH], h), kg_ref)
    kT_ref[0] = k
    ktok_ref[0] = k.T.astype(BF16)
    v = _dot_nt(wf_ref[2 * FOX_WIDTH:3 * FOX_WIDTH], h)
    vT_ref[0] = v
    vTb_ref[0] = v.astype(BF16)
    gkT_ref[0] = _dot_nt(wf_ref[3 * FOX_WIDTH:4 * FOX_WIDTH], h).astype(BF16)
    small = _dot_nt(wf_ref[_WF_SMALL:_WF_ROWS], h)
    lf = _log_sigmoid(small[0:FOX_HEADS] + _lane_tile(fb_ref[...], nrep))
    lfT_ref[0] = lf

    pieces = jnp.concatenate(_split3(lf), axis=0).astype(BF16)
    ri = lax.broadcasted_iota(jnp.int32, (tm, tm), 0)
    ci = lax.broadcasted_iota(jnp.int32, (tm, tm), 1)
    utri = jnp.where(ri <= ci, 1.0, 0.0).astype(BF16)
    cum3 = _dot(pieces, utri)
    tot3 = _dot(pieces, jnp.ones((tm, LANES), BF16))
    cum = cum3[0:8] + cum3[8:16] + cum3[16:24]
    tot = tot3[0:8] + tot3[8:16] + tot3[16:24]
    c = cum + _lane_tile(carry_ref[...], nrep)
    carry_ref[...] = carry_ref[...] + tot
    cpieces = jnp.concatenate(list(_split3(c)) + [jnp.zeros((LANES - 24, tm), F32)], axis=0)
    cp_ref[0] = cpieces.T.astype(BF16)

    glr = small[FOX_HEADS:FOX_HEADS + GLA_RANK].astype(BF16)
    pre = _dot(wa2t_ref[...], glr) + _lane_tile(bat_ref[...], nrep)
    la = _log_sigmoid(pre) * (1.0 / GLA_TAU)
    laT_ref[0] = la
    la_ref[0] = la.T

    c0 = 0
    fg = _dot(h, wt_ref[:, c0:c0 + FOX_WIDTH]); c0 += FOX_WIDTH
    fgs_ref[0] = (fg * jax.nn.sigmoid(fg)).astype(BF16)
    gq = _dot(h, wt_ref[:, c0:c0 + GLA_KW]); c0 += GLA_KW
    gq_ref[0] = (gq * (GLA_DK ** -0.5)).astype(BF16)
    for half in range(2):
        gv = _dot(h, wt_ref[:, c0:c0 + 512]); c0 += 512
        gv_ref[0, :, half * 512:(half + 1) * 512] = gv.astype(BF16)
    for half in range(2):
        gg = _dot(h, wt_ref[:, c0:c0 + 512]); c0 += 512
        ggs_ref[0, :, half * 512:(half + 1) * 512] = (gg * jax.nn.sigmoid(gg)).astype(BF16)
    for half in range(2):
        ma = _dot(h, wt_ref[:, c0:c0 + 512]); c0 += 512
        sa_ref[0, :, half * 512:(half + 1) * 512] = jax.nn.sigmoid(ma).astype(BF16)
    for half in range(2):
        mb = _dot(h, wt_ref[:, c0:c0 + 512]); c0 += 512
        sb_ref[0, :, half * 512:(half + 1) * 512] = jax.nn.sigmoid(mb).astype(BF16)


def _const_spec(shape):
    nd = len(shape)
    return pl.BlockSpec(shape, lambda *_: (0,) * nd, pipeline_mode=pl.Buffered(1))


def _proj(x, wts, tm):
    B, L, D = x.shape
    assert L % tm == 0 and tm % LANES == 0
    grid = (B, L // tm)
    tok = lambda w, dt: (jax.ShapeDtypeStruct((B, L, w), dt), pl.BlockSpec((1, tm, w), lambda b, i: (b, i, 0)))
    feat = lambda r, dt: (jax.ShapeDtypeStruct((B, r, L), dt), pl.BlockSpec((1, r, tm), lambda b, i: (b, 0, i)))
    outs = dict(
        qT=feat(FOX_WIDTH, BF16), kT=feat(FOX_WIDTH, F32), ktok=tok(FOX_WIDTH, BF16), cp=tok(LANES, BF16),
        vT=feat(FOX_WIDTH, F32), vTb=feat(FOX_WIDTH, BF16), lfT=feat(FOX_HEADS, F32), fgs=tok(FOX_WIDTH, BF16),
        gq=tok(GLA_KW, BF16), gkT=feat(GLA_KW, BF16), gv=tok(GLA_VW, BF16), la=tok(GLA_KW, F32),
        laT=feat(GLA_KW, F32), ggs=tok(GLA_VW, BF16), sa=tok(D_MODEL, BF16), sb=tok(D_MODEL, BF16))
    names = list(outs)
    res = pl.pallas_call(
        functools.partial(_proj_kernel, tm=tm),
        grid=grid,
        in_specs=[pl.BlockSpec((1, tm, D), lambda b, i: (b, i, 0)),
                  _const_spec((1, D)), _const_spec((_WF_ROWS, D)), _const_spec((D, _WT_COLS)),
                  _const_spec((FOX_WIDTH, LANES)), _const_spec((FOX_WIDTH, LANES)),
                  _const_spec((FOX_HEADS, LANES)), _const_spec((GLA_KW, GLA_RANK)),
                  _const_spec((GLA_KW, LANES))],
        out_specs=[outs[n][1] for n in names],
        out_shape=[outs[n][0] for n in names],
        scratch_shapes=[pltpu.VMEM((FOX_HEADS, LANES), F32)],
        compiler_params=pltpu.CompilerParams(
            dimension_semantics=("parallel", "arbitrary"), vmem_limit_bytes=VMEM_LIMIT),
        name="proj",
    )(x, wts["ln_g"], wts["wf"], wts["wt"], wts["qg"], wts["kg"], wts["fb"], wts["wa2t"], wts["bat"])
    return dict(zip(names, res))


def _prep_weights(ln_g, w_in, fox_b_f, q_norm_g, k_norm_g, gla_w_a2, gla_b_a):
    w = w_in
    sl = lambda o, n: w[:, o:o + n]
    small = jnp.concatenate([sl(_FF, FOX_HEADS), sl(_GLR, GLA_RANK), jnp.zeros((D_MODEL, 8), F32)], axis=1)
    wf = jnp.concatenate([sl(_FQ, FOX_WIDTH), sl(_FK, FOX_WIDTH), sl(_FV, FOX_WIDTH), sl(_GK, GLA_KW), small],
                         axis=1).T.astype(BF16)
    wt = jnp.concatenate([sl(_FG, FOX_WIDTH), sl(_GQ, GLA_KW), sl(_GV, GLA_VW), sl(_GG, GLA_VW),
                          sl(_MA, D_MODEL), sl(_MB, D_MODEL)], axis=1).astype(BF16)
    rep = lambda vec: jnp.broadcast_to(vec[:, None], (vec.shape[0], LANES)).astype(F32)
    return dict(ln_g=ln_g[None, :], wf=wf, wt=wt,
                qg=rep(jnp.tile(q_norm_g, FOX_HEADS)), kg=rep(jnp.tile(k_norm_g, FOX_HEADS)),
                fb=rep(fox_b_f), wa2t=gla_w_a2.T.astype(BF16), bat=rep(gla_b_a))


NEG_BIG = -1e30
N_PAIRS = FOX_HEADS // 2


def _fox_kernel(qT_ref, ktok_ref, cp_ref, vTb_ref, fgs_ref, o_ref, m_ref, l_ref, acc_ref, oT_ref, *, tq):
    qi = pl.program_id(1)
    rows = lax.broadcasted_iota(jnp.int32, (LANES, tq), 0)
    kpos = lax.broadcasted_iota(jnp.int32, (tq, 2 * tq), 0)
    qpos = lax.broadcasted_iota(jnp.int32, (tq, 2 * tq), 1) % tq
    causal = kpos <= qpos
    z64 = jnp.zeros((FOX_HEAD_DIM, tq), BF16)

    for p in range(N_PAIRS):
        lo = p * LANES

        def bias_rows(h):
            hit = (rows == h) | (rows == h + 8) | (rows == h + 16)
            return jnp.where(hit, -1.0, 0.0).astype(BF16)

        qe = qT_ref[0, lo:lo + FOX_HEAD_DIM, :]
        qo = qT_ref[0, lo + FOX_HEAD_DIM:lo + LANES, :]
        wq = jnp.concatenate(
            [jnp.concatenate([qe, z64, bias_rows(2 * p)], axis=0),
             jnp.concatenate([z64, qo, bias_rows(2 * p + 1)], axis=0)], axis=1)

        m_ref[...] = jnp.full_like(m_ref, NEG_BIG)
        l_ref[...] = jnp.zeros_like(l_ref)
        acc_ref[...] = jnp.zeros_like(acc_ref)

        def step(kj, masked):
            k0 = pl.multiple_of(kj * tq, tq)
            ka = jnp.concatenate([ktok_ref[0, pl.ds(k0, tq), lo:lo + LANES],
                                  cp_ref[0, pl.ds(k0, tq), :]], axis=1)
            s = _dot(ka, wq)
            if masked:
                s = jnp.where(causal, s, NEG_BIG)
            m_prev = m_ref[...]
            m_new = jnp.maximum(m_prev, jnp.max(s, axis=0, keepdims=True))
            alpha = jnp.exp(m_prev - m_new)
            pT = jnp.exp(s - m_new)
            l_ref[...] = alpha * l_ref[...] + jnp.sum(pT, axis=0, keepdims=True)
            m_ref[...] = m_new
            pb = pT.astype(BF16)
            ve = vTb_ref[0, lo:lo + FOX_HEAD_DIM, pl.ds(k0, tq)]
            vo = vTb_ref[0, lo + FOX_HEAD_DIM:lo + LANES, pl.ds(k0, tq)]
            acc_ref[0:FOX_HEAD_DIM] = alpha[:, :tq] * acc_ref[0:FOX_HEAD_DIM] + _dot(ve, pb[:, :tq])
            acc_ref[FOX_HEAD_DIM:] = alpha[:, tq:] * acc_ref[FOX_HEAD_DIM:] + _dot(vo, pb[:, tq:])

        def body(kj, carry):
            step(kj, False)
            return carry

        lax.fori_loop(0, qi, body, 0)
        step(qi, True)

        inv = 1.0 / l_ref[...]
        oT_ref[lo:lo + FOX_HEAD_DIM] = acc_ref[0:FOX_HEAD_DIM] * inv[:, :tq]
        oT_ref[lo + FOX_HEAD_DIM:lo + LANES] = acc_ref[FOX_HEAD_DIM:] * inv[:, tq:]

    o_ref[0] = (oT_ref[...].T * fgs_ref[0].astype(F32)).astype(BF16)


def _fox(p, tq):
    B, _, L = p["qT"].shape
    assert L % tq == 0
    return pl.pallas_call(
        functools.partial(_fox_kernel, tq=tq),
        grid=(B, L // tq),
        in_specs=[pl.BlockSpec((1, FOX_WIDTH, tq), lambda b, i: (b, 0, i)),
                  pl.BlockSpec((1, L, FOX_WIDTH), lambda b, i: (b, 0, 0)),
                  pl.BlockSpec((1, L, LANES), lambda b, i: (b, 0, 0)),
                  pl.BlockSpec((1, FOX_WIDTH, L), lambda b, i: (b, 0, 0)),
                  pl.BlockSpec((1, tq, FOX_WIDTH), lambda b, i: (b, i, 0))],
        out_specs=pl.BlockSpec((1, tq, FOX_WIDTH), lambda b, i: (b, i, 0)),
        out_shape=jax.ShapeDtypeStruct((B, L, FOX_WIDTH), BF16),
        scratch_shapes=[pltpu.VMEM((1, 2 * tq), F32), pltpu.VMEM((1, 2 * tq), F32),
                        pltpu.VMEM((LANES, tq), F32), pltpu.VMEM((FOX_WIDTH, tq), F32)],
        compiler_params=pltpu.CompilerParams(
            dimension_semantics=("parallel", "arbitrary"), vmem_limit_bytes=VMEM_LIMIT),
        name="fox",
    )(p["qT"], p["ktok"], p["cp"], p["vTb"], p["fgs"])


GLA_C = 128
GLA_MID = GLA_C // 2 - 1


def _gla_consts():
    i = jnp.arange(GLA_C)
    incl = (i[:, None] <= i[None, :]).astype(F32)
    upto_mid = (i[:, None] <= GLA_MID).astype(F32)
    uc = incl - upto_mid
    ud = 1.0 - incl
    ones = jnp.ones((GLA_C, LANES), F32)
    wfeat = jnp.concatenate([uc, ud, ones], axis=1)
    wfeat3 = jnp.concatenate([wfeat] * 3, axis=0).astype(BF16)
    ltok = jnp.concatenate([uc.T, incl.T], axis=0)
    ltok3 = jnp.concatenate([ltok] * 3, axis=1).astype(BF16)
    return wfeat3, ltok3


def _gla_kernel(gq_ref, gkT_ref, gv_ref, la_ref, laT_ref, ggs_ref, gng_ref, wfeat_ref, ltok_ref,
                s0_ref, og_ref, s_ref):
    C = GLA_C

    @pl.when(pl.program_id(1) == 0)
    def _():
        s_ref[...] = s0_ref[...]

    g3 = jnp.concatenate(_split3(la_ref[0]), axis=0).astype(BF16)
    btok = _dot(ltok_ref[...], g3)
    bc, b = btok[0:C], btok[C:2 * C]
    gT3 = jnp.concatenate(_split3(laT_ref[0]), axis=1).astype(BF16)
    bfeat = _dot(gT3, wfeat_ref[...])
    bcT, dT, blast = bfeat[:, 0:C], bfeat[:, C:2 * C], bfeat[:, 2 * C:]

    gq = gq_ref[0].astype(F32)
    q_in = (gq * jnp.exp(bc)).astype(BF16)
    q_s = (gq * jnp.exp(b)).astype(BF16)
    gkT = gkT_ref[0].astype(F32)
    k_inT = (gkT * jnp.exp(-bcT)).astype(BF16)
    k_decT = (gkT * jnp.exp(dT)).astype(BF16)
    decay = jnp.exp(blast)

    ti = lax.broadcasted_iota(jnp.int32, (C, C), 0)
    si = lax.broadcasted_iota(jnp.int32, (C, C), 1)
    tril = si <= ti
    gng = gng_ref[...]
    for h in range(GLA_HEADS):
        ks = slice(h * GLA_DK, (h + 1) * GLA_DK)
        vs = slice(h * GLA_DV, (h + 1) * GLA_DV)
        att = jnp.where(tril, _dot(q_in[:, ks], k_inT[ks, :]), 0.0).astype(BF16)
        vh = gv_ref[0, :, vs]
        s_h = s_ref[0, h]
        o = _dot(att, vh) + _dot(q_s[:, ks], s_h.astype(BF16))
        s_ref[0, h] = jnp.concatenate([decay[ks]] * 2, axis=1) * s_h + _dot(k_decT[ks, :], vh)
        ms = jnp.mean(o * o, axis=-1, keepdims=True)
        on = (o * lax.rsqrt(ms + EPS)) * gng
        og_ref[0, :, vs] = (on * ggs_ref[0, :, vs].astype(F32)).astype(BF16)


def _gla(p, s0, gla_norm_g):
    B, L, _ = p["gq"].shape
    C = GLA_C
    assert L % C == 0
    wfeat3, ltok3 = _gla_consts()
    tok = lambda w: pl.BlockSpec((1, C, w), lambda b, i: (b, i, 0))
    feat = lambda r: pl.BlockSpec((1, r, C), lambda b, i: (b, 0, i))
    st = pl.BlockSpec((1, GLA_HEADS, GLA_DK, GLA_DV), lambda b, i: (b, 0, 0, 0))
    return pl.pallas_call(
        _gla_kernel,
        grid=(B, L // C),
        in_specs=[tok(GLA_KW), feat(GLA_KW), tok(GLA_VW), tok(GLA_KW), feat(GLA_KW), tok(GLA_VW),
                  _const_spec((1, GLA_DV)), _const_spec(wfeat3.shape), _const_spec(ltok3.shape), st],
        out_specs=[tok(GLA_VW), st],
        out_shape=[jax.ShapeDtypeStruct((B, L, GLA_VW), BF16),
                   jax.ShapeDtypeStruct((B, GLA_HEADS, GLA_DK, GLA_DV), F32)],
        compiler_params=pltpu.CompilerParams(
            dimension_semantics=("parallel", "arbitrary"), vmem_limit_bytes=VMEM_LIMIT),
        name="gla",
    )(p["gq"], p["gkT"], p["gv"], p["la"], p["laT"], p["ggs"], gla_norm_g[None, :], wfeat3, ltok3, s0)


def _merge_kernel(x_ref, of_ref, og_ref, sa_ref, sb_ref, wa_ref, wb_ref, wo_ref, y_ref):
    a = _dot(of_ref[0], wa_ref[...])
    b = _dot(og_ref[0], wb_ref[...])
    m = sa_ref[0].astype(F32) * a + sb_ref[0].astype(F32) * b
    y_ref[0] = x_ref[0] + _dot(m.astype(BF16), wo_ref[...])


def _merge(x, of, og, sa, sb, w_up_a, w_up_b, w_out, tm):
    B, L, D = x.shape
    assert L % tm == 0
    tok = lambda w: pl.BlockSpec((1, tm, w), lambda b, i: (b, i, 0))
    return pl.pallas_call(
        _merge_kernel,
        grid=(B, L // tm),
        in_specs=[tok(D), tok(FOX_WIDTH), tok(GLA_VW), tok(D), tok(D),
                  _const_spec((FOX_WIDTH, D)), _const_spec((GLA_VW, D)), _const_spec((D, D))],
        out_specs=tok(D),
        out_shape=jax.ShapeDtypeStruct((B, L, D), F32),
        compiler_params=pltpu.CompilerParams(
            dimension_semantics=("parallel", "parallel"), vmem_limit_bytes=VMEM_LIMIT),
        name="merge",
    )(x, of, og, sa, sb, w_up_a.astype(BF16), w_up_b.astype(BF16), w_out.astype(BF16))


def _prompt_layer(x, wts, gla_norm_g, w_up_a, w_up_b, w_out):
    B, L, _ = x.shape
    p = _proj(x, wts, 256)
    of = _fox(p, 256)
    s0 = jnp.zeros((B, GLA_HEADS, GLA_DK, GLA_DV), F32)
    og, s_end = _gla(p, s0, gla_norm_g)
    y = _merge(x, of, og, p["sa"], p["sb"], w_up_a, w_up_b, w_out, 512)
    k_out = p["kT"].reshape(B, FOX_HEADS, FOX_HEAD_DIM, L).transpose(0, 3, 1, 2)
    v_out = p["vT"].reshape(B, FOX_HEADS, FOX_HEAD_DIM, L).transpose(0, 3, 1, 2)
    lf_out = p["lfT"].transpose(0, 2, 1)
    return y, k_out, v_out, lf_out, s_end


DEC_PP = 8


def _fox_dec_kernel(pt_ref, *refs, T):
    PP = DEC_PP
    k_refs, v_refs, lf_refs = refs[0:PP], refs[PP:2 * PP], refs[2 * PP:3 * PP]
    q_ref, knT_ref, vn_ref, lfn_ref, fgs_ref = refs[3 * PP:3 * PP + 5]
    o_ref = refs[3 * PP + 5]
    m_ref, l_ref, acc_ref, carry_ref = refs[3 * PP + 6:]
    j = pl.program_id(1)
    H, P = FOX_HEADS, PAGE_SIZE

    @pl.when(j == 0)
    def _():
        m_ref[...] = jnp.full_like(m_ref, NEG_BIG)
        l_ref[...] = jnp.zeros_like(l_ref)
        acc_ref[...] = jnp.zeros_like(acc_ref)
        carry_ref[...] = jnp.zeros_like(carry_ref)

    lf = jnp.concatenate([r[0] for r in lf_refs], axis=0)
    pieces = jnp.concatenate(_split3(lf), axis=0).astype(BF16)
    ri = lax.broadcasted_iota(jnp.int32, (P, 2 * P), 0)
    ci = lax.broadcasted_iota(jnp.int32, (P, 2 * P), 1)
    wcum = jnp.where((ri <= ci) | (ci >= P), 1.0, 0.0).astype(BF16)
    cw3 = _dot(pieces, wcum)
    n = PP * H
    cw = cw3[0:n] + cw3[n:2 * n] + cw3[2 * n:3 * n]
    off = carry_ref[...]
    c_pages = []
    for i in range(PP):
        c_pages.append(cw[i * H:(i + 1) * H, 0:P] + off)
        off = off + cw[i * H:(i + 1) * H, P:2 * P]
    carry_ref[...] = off

    def online(h, s, v_dot):
        m_prev = m_ref[h][:, 0:1]
        l_prev = l_ref[h][:, 0:1]
        m_new = jnp.maximum(m_prev, jnp.max(s, axis=-1, keepdims=True))
        alpha = jnp.exp(m_prev - m_new)
        p = jnp.exp(s - m_new)
        l_new = alpha * l_prev + jnp.sum(p, axis=-1, keepdims=True)
        acc_ref[h] = alpha * acc_ref[h] + v_dot(p.astype(BF16))
        m_ref[h] = jnp.broadcast_to(m_new, (T, LANES))
        l_ref[h] = jnp.broadcast_to(l_new, (T, LANES))

    for h in range(H):
        qh = q_ref[0, h].astype(BF16)
        kh = jnp.concatenate([r[0, h] for r in k_refs], axis=1).astype(BF16)
        vh = jnp.concatenate([r[0, h] for r in v_refs], axis=1).astype(BF16)
        bias = jnp.concatenate([jnp.broadcast_to(c[h:h + 1, :], (T, P)) for c in c_pages], axis=1)
        s = _dot(qh, kh) - bias
        online(h, s, lambda pb: _dot_nt(pb, vh))

    @pl.when(j == pl.num_programs(1) - 1)
    def _():
        lfn = lfn_ref[0]
        pn = jnp.concatenate(_split3(lfn), axis=0).astype(BF16)
        r8 = lax.broadcasted_iota(jnp.int32, (T, T), 0)
        c8 = lax.broadcasted_iota(jnp.int32, (T, T), 1)
        u8 = jnp.where(r8 <= c8, 1.0, 0.0).astype(BF16)
        cn3 = _dot(pn, u8)
        c_new = carry_ref[:, 0:T] + cn3[0:H] + cn3[H:2 * H] + cn3[2 * H:3 * H]
        outs = []
        for h in range(H):
            qh = q_ref[0, h].astype(BF16)
            s = _dot(qh, knT_ref[0, h].astype(BF16)) - c_new[h:h + 1, :]
            s = jnp.where(c8 <= r8, s, NEG_BIG)
            vnh = vn_ref[0, h].astype(BF16)
            online(h, s, lambda pb: _dot(pb, vnh))
            outs.append(acc_ref[h] / l_ref[h][:, 0:1])
        o_ref[0] = jnp.concatenate(outs, axis=1) * fgs_ref[0]


def _fox_dec(page_table, ck, cv, clf, q4, knT, vn, lfn, fgs4):
    Bd, n_pages = page_table.shape
    PP = DEC_PP
    assert n_pages % PP == 0
    T = q4.shape[2]
    H, HD, P = FOX_HEADS, FOX_HEAD_DIM, PAGE_SIZE

    def page4(i):
        return pl.BlockSpec((1, H, HD, P), lambda b, j, pt: (pt[b, j * PP + i], 0, 0, 0))

    def page3(i):
        return pl.BlockSpec((1, H, P), lambda b, j, pt: (pt[b, j * PP + i], 0, 0))

    per_b4 = lambda s: pl.BlockSpec((1,) + s, lambda b, j, pt: (b, 0, 0, 0))
    per_b3 = lambda s: pl.BlockSpec((1,) + s, lambda b, j, pt: (b, 0, 0))
    in_specs = ([page4(i) for i in range(PP)] + [page4(i) for i in range(PP)] + [page3(i) for i in range(PP)]
                + [per_b4((H, T, HD)), per_b4((H, HD, T)), per_b4((H, T, HD)), per_b3((H, T)),
                   per_b3((T, FOX_WIDTH))])
    grid_spec = pltpu.PrefetchScalarGridSpec(
        num_scalar_prefetch=1,
        grid=(Bd, n_pages // PP),
        in_specs=in_specs,
        out_specs=per_b3((T, FOX_WIDTH)),
        scratch_shapes=[pltpu.VMEM((H, T, LANES), F32), pltpu.VMEM((H, T, LANES), F32),
                        pltpu.VMEM((H, T, HD), F32), pltpu.VMEM((H, LANES), F32)])
    return pl.pallas_call(
        functools.partial(_fox_dec_kernel, T=T),
        grid_spec=grid_spec,
        out_shape=jax.ShapeDtypeStruct((Bd, T, FOX_WIDTH), F32),
        compiler_params=pltpu.CompilerParams(
            dimension_semantics=("parallel", "arbitrary"), vmem_limit_bytes=VMEM_LIMIT),
        name="fox_dec",
    )(page_table, *([ck] * PP), *([cv] * PP), *([clf] * PP), q4, knT, vn, lfn, fgs4)


def _gla_dec_kernel(gq_ref, gkT_ref, gv_ref, la_ref, laT_ref, ggs_ref, gng_ref, s0_ref, og_ref, s_ref, *, T):
    r8 = lax.broadcasted_iota(jnp.int32, (T, T), 0)
    c8 = lax.broadcasted_iota(jnp.int32, (T, T), 1)
    low = jnp.where(c8 <= r8, 1.0, 0.0).astype(BF16)
    g3 = jnp.concatenate(_split3(la_ref[0]), axis=0).astype(BF16)
    b = _dot(jnp.concatenate([low] * 3, axis=1), g3)
    gT3 = jnp.concatenate(_split3(laT_ref[0]), axis=1).astype(BF16)
    ri = lax.broadcasted_iota(jnp.int32, (T, T + LANES), 0)
    ci = lax.broadcasted_iota(jnp.int32, (T, T + LANES), 1)
    wf = jnp.where((ri <= ci) | (ci >= T), 1.0, 0.0).astype(BF16)
    bf = _dot(gT3, jnp.concatenate([wf] * 3, axis=0))
    bT, blast = bf[:, 0:T], bf[:, T:]
    q_in = (gq_ref[0] * jnp.exp(b)).astype(BF16)
    gkT = gkT_ref[0]
    k_inT = (gkT * jnp.exp(-bT)).astype(BF16)
    k_decT = (gkT * jnp.exp(blast[:, 0:T] - bT)).astype(BF16)
    decay = jnp.exp(blast)
    gng = gng_ref[...]
    for h in range(GLA_HEADS):
        ks = slice(h * GLA_DK, (h + 1) * GLA_DK)
        vs = slice(h * GLA_DV, (h + 1) * GLA_DV)
        att = jnp.where(c8 <= r8, _dot(q_in[:, ks], k_inT[ks, :]), 0.0).astype(BF16)
        vh = gv_ref[0, :, vs].astype(BF16)
        s_h = s0_ref[0, h]
        o = _dot(att, vh) + _dot(q_in[:, ks], s_h.astype(BF16))
        s_ref[0, h] = jnp.concatenate([decay[ks]] * 2, axis=1) * s_h + _dot(k_decT[ks, :], vh)
        ms = jnp.mean(o * o, axis=-1, keepdims=True)
        og_ref[0, :, vs] = ((o * lax.rsqrt(ms + EPS)) * gng) * ggs_ref[0, :, vs]


def _gla_dec(gq, gkT, gv, la, laT, ggs, gla_norm_g, s0):
    Bd, T, _ = gq.shape
    tok = lambda w: pl.BlockSpec((1, T, w), lambda b: (b, 0, 0))
    feat = lambda r: pl.BlockSpec((1, r, T), lambda b: (b, 0, 0))
    st = pl.BlockSpec((1, GLA_HEADS, GLA_DK, GLA_DV), lambda b: (b, 0, 0, 0))
    return pl.pallas_call(
        functools.partial(_gla_dec_kernel, T=T),
        grid=(Bd,),
        in_specs=[tok(GLA_KW), feat(GLA_KW), tok(GLA_VW), tok(GLA_KW), feat(GLA_KW), tok(GLA_VW),
                  _const_spec((1, GLA_DV)), st],
        out_specs=[tok(GLA_VW), st],
        out_shape=[jax.ShapeDtypeStruct((Bd, T, GLA_VW), F32),
                   jax.ShapeDtypeStruct((Bd, GLA_HEADS, GLA_DK, GLA_DV), F32)],
        compiler_params=pltpu.CompilerParams(
            dimension_semantics=("parallel",), vmem_limit_bytes=VMEM_LIMIT),
        name="gla_dec",
    )(gq, gkT, gv, la, laT, ggs, gla_norm_g[None, :], s0)


def _sample_layer(x, wts, cache_k, cache_v, cache_logf, state, page_table, gla_norm_g, w_up_a, w_up_b, w_out):
    Bd, T, D = x.shape
    N = Bd * T
    H, HD = FOX_HEADS, FOX_HEAD_DIM
    p = _proj(x.reshape(1, N, D), wts, N)
    feat_bt = lambda a, r: a[0].astype(F32).reshape(r, Bd, T)
    tok_bt = lambda a: a[0].astype(F32).reshape(Bd, T, -1)
    qf = feat_bt(p["qT"], FOX_WIDTH).reshape(H, HD, Bd, T)
    kf = feat_bt(p["kT"], FOX_WIDTH).reshape(H, HD, Bd, T)
    vf = feat_bt(p["vT"], FOX_WIDTH).reshape(H, HD, Bd, T)
    lff = feat_bt(p["lfT"], H)
    ck = jnp.transpose(cache_k, (0, 2, 3, 1))
    cv = jnp.transpose(cache_v, (0, 2, 3, 1))
    clf = jnp.transpose(cache_logf, (0, 2, 1))
    of = _fox_dec(page_table, ck, cv, clf,
                  qf.transpose(2, 0, 3, 1), kf.transpose(2, 0, 1, 3), vf.transpose(2, 0, 3, 1),
                  lff.transpose(1, 0, 2), tok_bt(p["fgs"]))
    og, s_end = _gla_dec(tok_bt(p["gq"]), feat_bt(p["gkT"], GLA_KW).transpose(1, 0, 2), tok_bt(p["gv"]),
                         tok_bt(p["la"]), feat_bt(p["laT"], GLA_KW).transpose(1, 0, 2), tok_bt(p["ggs"]),
                         gla_norm_g, state)
    y = _merge(x.reshape(1, N, D), of.reshape(1, N, FOX_WIDTH).astype(BF16), og.reshape(1, N, GLA_VW).astype(BF16),
               p["sa"], p["sb"], w_up_a, w_up_b, w_out, N)
    k_out = kf.transpose(2, 3, 0, 1)
    v_out = vf.transpose(2, 3, 0, 1)
    lf_out = lff.transpose(1, 2, 0)
    return y.reshape(Bd, T, D), k_out, v_out, lf_out, s_end


def kernel(x_prompt, x_sample, cache_k, cache_v, cache_logf, state_gla, page_table, ln_g, w_in, fox_b_f, q_norm_g, k_norm_g, gla_w_a2, gla_b_a, gla_norm_g, w_up_a, w_up_b, w_out):
    wts = _prep_weights(ln_g[0], w_in[0], fox_b_f[0], q_norm_g[0], k_norm_g[0], gla_w_a2[0], gla_b_a[0])
    yp, kp, vp, lfp, sp = _prompt_layer(x_prompt, wts, gla_norm_g[0], w_up_a[0], w_up_b[0], w_out[0])
    ys, kq, vq, lfq, sq = _sample_layer(x_sample, wts, cache_k[0], cache_v[0], cache_logf[0], state_gla[0],
                                        page_table, gla_norm_g[0], w_up_a[0], w_up_b[0], w_out[0])
    return (yp, ys, kp[None], vp[None], lfp[None], sp[None], kq[None], vq[None], lfq[None], sq[None])
```

```python
import functools

import jax
import jax.numpy as jnp
from jax import lax
from jax.experimental import pallas as pl
from jax.experimental.pallas import tpu as pltpu

F32 = jnp.float32
BF16 = jnp.bfloat16

D_MODEL = 1024
FOX_HEADS = 8
FOX_HEAD_DIM = 64
FOX_WIDTH = FOX_HEADS * FOX_HEAD_DIM
FOX_SCALE = FOX_HEAD_DIM ** -0.5
LOG2E = 1.4426950408889634
V_AUG = FOX_HEAD_DIM + 16
GLA_HEADS = 4
GLA_DK = 128
GLA_DV = 256
GLA_KW = GLA_HEADS * GLA_DK
GLA_VW = GLA_HEADS * GLA_DV
GLA_RANK = 16
GLA_TAU = 16.0
EPS = 1e-6
PAGE_SIZE = 128

LANES = 128
SUBLANES = 8
VMEM_LIMIT = 48 * 1024 * 1024

_SIZES = (FOX_WIDTH, FOX_WIDTH, FOX_WIDTH, FOX_HEADS, FOX_WIDTH,
          GLA_KW, GLA_KW, GLA_VW, GLA_RANK, GLA_VW, D_MODEL, D_MODEL)
_OFF = [0]
for _s in _SIZES:
    _OFF.append(_OFF[-1] + _s)
(_FQ, _FK, _FV, _FF, _FG, _GQ, _GK, _GV, _GLR, _GG, _MA, _MB) = _OFF[:-1]

_WF_SMALL = 4 * FOX_WIDTH
_WF_ROWS = _WF_SMALL + 32
_WT_COLS = FOX_WIDTH + GLA_KW + GLA_VW + GLA_VW + 2 * D_MODEL


def _dot_nt(a, b):
    return lax.dot_general(a, b, (((1,), (1,)), ((), ())), preferred_element_type=F32)


def _dot(a, b):
    return jnp.dot(a, b, preferred_element_type=F32)


def _log_sigmoid(x):
    return -(jnp.maximum(-x, 0.0) + jnp.log1p(jnp.exp(-jnp.abs(x))))


def _split3(a):
    hi = a.astype(BF16).astype(F32)
    r = a - hi
    mid = r.astype(BF16).astype(F32)
    lo = (r - mid).astype(BF16).astype(F32)
    return hi, mid, lo


def _lane_tile(a, n):
    return a if n == 1 else jnp.concatenate([a] * n, axis=1)


def _proj_kernel(x_ref, lng_ref, wf_ref, wt_ref, qg_ref, kg_ref, fb_ref, wa2t_ref, bat_ref,
                 qT_ref, kT_ref, ktok_ref, cp_ref, vT_ref, vTb_ref, lfT_ref, fgs_ref,
                 gq_ref, gkT_ref, gv_ref, la_ref, laT_ref, ggs_ref, sa_ref, sb_ref,
                 carry_ref, *, tm):
    nrep = tm // LANES

    @pl.when(pl.program_id(1) == 0)
    def _():
        carry_ref[...] = jnp.zeros_like(carry_ref)

    x = x_ref[0]
    ms = jnp.mean(x * x, axis=-1, keepdims=True)
    h = ((x * lax.rsqrt(ms + EPS)) * lng_ref[...]).astype(BF16)

    def headnorm(t, g_ref):
        outs = []
        for hh in range(FOX_HEADS):
            blk = t[hh * FOX_HEAD_DIM:(hh + 1) * FOX_HEAD_DIM]
            ssq = jnp.sum(blk * blk, axis=0, keepdims=True) * (1.0 / FOX_HEAD_DIM)
            g = _lane_tile(g_ref[hh * FOX_HEAD_DIM:(hh + 1) * FOX_HEAD_DIM], nrep)
            outs.append((blk * lax.rsqrt(ssq + EPS)) * g)
        return jnp.concatenate(outs, axis=0)

    q = headnorm(_dot_nt(wf_ref[0:FOX_WIDTH], h), qg_ref)
    qT_ref[0] = (q * (FOX_SCALE * LOG2E)).astype(BF16)
    k = headnorm(_dot_nt(wf_ref[FOX_WIDTH:2 * FOX_WIDTH], h), kg_ref)
    kT_ref[0] = k
    ktok_ref[0] = k.T.astype(BF16)
    v = _dot_nt(wf_ref[2 * FOX_WIDTH:3 * FOX_WIDTH], h)
    vT_ref[0] = v
    ones_rows = jnp.where(lax.broadcasted_iota(jnp.int32, (V_AUG - FOX_HEAD_DIM, tm), 0) == 0, 1.0, 0.0)
    vaug = []
    for hh in range(FOX_HEADS):
        vaug += [v[hh * FOX_HEAD_DIM:(hh + 1) * FOX_HEAD_DIM], ones_rows]
    vTb_ref[0] = jnp.concatenate(vaug, axis=0).astype(BF16)
    gkT_ref[0] = _dot_nt(wf_ref[3 * FOX_WIDTH:4 * FOX_WIDTH], h).astype(BF16)
    small = _dot_nt(wf_ref[_WF_SMALL:_WF_ROWS], h)
    lf = _log_sigmoid(small[0:FOX_HEADS] + _lane_tile(fb_ref[...], nrep))
    lfT_ref[0] = lf

    pieces = jnp.concatenate(_split3(lf), axis=0).astype(BF16)
    ri = lax.broadcasted_iota(jnp.int32, (tm, tm), 0)
    ci = lax.broadcasted_iota(jnp.int32, (tm, tm), 1)
    utri = jnp.where(ri <= ci, 1.0, 0.0).astype(BF16)
    cum3 = _dot(pieces, utri)
    tot3 = _dot(pieces, jnp.ones((tm, LANES), BF16))
    cum = cum3[0:8] + cum3[8:16] + cum3[16:24]
    tot = tot3[0:8] + tot3[8:16] + tot3[16:24]
    c = cum + _lane_tile(carry_ref[...], nrep)
    carry_ref[...] = carry_ref[...] + tot
    cpieces = jnp.concatenate(list(_split3(c * LOG2E)) + [jnp.zeros((LANES - 24, tm), F32)], axis=0)
    cp_ref[0] = cpieces.T.astype(BF16)

    glr = small[FOX_HEADS:FOX_HEADS + GLA_RANK].astype(BF16)
    pre = _dot(wa2t_ref[...], glr) + _lane_tile(bat_ref[...], nrep)
    la = _log_sigmoid(pre) * (1.0 / GLA_TAU)
    laT_ref[0] = la
    la_ref[0] = la.T

    c0 = 0
    fg = _dot(h, wt_ref[:, c0:c0 + FOX_WIDTH]); c0 += FOX_WIDTH
    fgs_ref[0] = (fg * jax.nn.sigmoid(fg)).astype(BF16)
    gq = _dot(h, wt_ref[:, c0:c0 + GLA_KW]); c0 += GLA_KW
    gq_ref[0] = (gq * (GLA_DK ** -0.5)).astype(BF16)
    for half in range(2):
        gv = _dot(h, wt_ref[:, c0:c0 + 512]); c0 += 512
        gv_ref[0, :, half * 512:(half + 1) * 512] = gv.astype(BF16)
    for half in range(2):
        gg = _dot(h, wt_ref[:, c0:c0 + 512]); c0 += 512
        ggs_ref[0, :, half * 512:(half + 1) * 512] = (gg * jax.nn.sigmoid(gg)).astype(BF16)
    for half in range(2):
        ma = _dot(h, wt_ref[:, c0:c0 + 512]); c0 += 512
        sa_ref[0, :, half * 512:(half + 1) * 512] = jax.nn.sigmoid(ma).astype(BF16)
    for half in range(2):
        mb = _dot(h, wt_ref[:, c0:c0 + 512]); c0 += 512
        sb_ref[0, :, half * 512:(half + 1) * 512] = jax.nn.sigmoid(mb).astype(BF16)


def _const_spec(shape):
    nd = len(shape)
    return pl.BlockSpec(shape, lambda *_: (0,) * nd, pipeline_mode=pl.Buffered(1))


def _proj(x, wts, tm):
    B, L, D = x.shape
    assert L % tm == 0 and tm % LANES == 0
    grid = (B, L // tm)
    tok = lambda w, dt: (jax.ShapeDtypeStruct((B, L, w), dt), pl.BlockSpec((1, tm, w), lambda b, i: (b, i, 0)))
    feat = lambda r, dt: (jax.ShapeDtypeStruct((B, r, L), dt), pl.BlockSpec((1, r, tm), lambda b, i: (b, 0, i)))
    outs = dict(
        qT=feat(FOX_WIDTH, BF16), kT=feat(FOX_WIDTH, F32), ktok=tok(FOX_WIDTH, BF16), cp=tok(LANES, BF16),
        vT=feat(FOX_WIDTH, F32), vTb=feat(FOX_HEADS * V_AUG, BF16), lfT=feat(FOX_HEADS, F32), fgs=tok(FOX_WIDTH, BF16),
        gq=tok(GLA_KW, BF16), gkT=feat(GLA_KW, BF16), gv=tok(GLA_VW, BF16), la=tok(GLA_KW, F32),
        laT=feat(GLA_KW, F32), ggs=tok(GLA_VW, BF16), sa=tok(D_MODEL, BF16), sb=tok(D_MODEL, BF16))
    names = list(outs)
    res = pl.pallas_call(
        functools.partial(_proj_kernel, tm=tm),
        grid=grid,
        in_specs=[pl.BlockSpec((1, tm, D), lambda b, i: (b, i, 0)),
                  _const_spec((1, D)), _const_spec((_WF_ROWS, D)), _const_spec((D, _WT_COLS)),
                  _const_spec((FOX_WIDTH, LANES)), _const_spec((FOX_WIDTH, LANES)),
                  _const_spec((FOX_HEADS, LANES)), _const_spec((GLA_KW, GLA_RANK)),
                  _const_spec((GLA_KW, LANES))],
        out_specs=[outs[n][1] for n in names],
        out_shape=[outs[n][0] for n in names],
        scratch_shapes=[pltpu.VMEM((FOX_HEADS, LANES), F32)],
        compiler_params=pltpu.CompilerParams(
            dimension_semantics=("parallel", "arbitrary"), vmem_limit_bytes=VMEM_LIMIT),
        name="proj",
    )(x, wts["ln_g"], wts["wf"], wts["wt"], wts["qg"], wts["kg"], wts["fb"], wts["wa2t"], wts["bat"])
    return dict(zip(names, res))


def _prep_weights(ln_g, w_in, fox_b_f, q_norm_g, k_norm_g, gla_w_a2, gla_b_a):
    w = w_in
    sl = lambda o, n: w[:, o:o + n]
    small = jnp.concatenate([sl(_FF, FOX_HEADS), sl(_GLR, GLA_RANK), jnp.zeros((D_MODEL, 8), F32)], axis=1)
    wf = jnp.concatenate([sl(_FQ, FOX_WIDTH), sl(_FK, FOX_WIDTH), sl(_FV, FOX_WIDTH), sl(_GK, GLA_KW), small],
                         axis=1).T.astype(BF16)
    wt = jnp.concatenate([sl(_FG, FOX_WIDTH), sl(_GQ, GLA_KW), sl(_GV, GLA_VW), sl(_GG, GLA_VW),
                          sl(_MA, D_MODEL), sl(_MB, D_MODEL)], axis=1).astype(BF16)
    rep = lambda vec: jnp.broadcast_to(vec[:, None], (vec.shape[0], LANES)).astype(F32)
    return dict(ln_g=ln_g[None, :], wf=wf, wt=wt,
                qg=rep(jnp.tile(q_norm_g, FOX_HEADS)), kg=rep(jnp.tile(k_norm_g, FOX_HEADS)),
                fb=rep(fox_b_f), wa2t=gla_w_a2.T.astype(BF16), bat=rep(gla_b_a))


NEG_BIG = -1e30
N_PAIRS = FOX_HEADS // 2


def _fox_kernel(qT_ref, ktok_ref, cp_ref, vTb_ref, fgs_ref, o_ref, wq_ref, m_ref, acc_ref,
                s0_ref, alpha3_ref, pb3_ref, *, tq):
    qi = pl.program_id(1)
    rows = lax.broadcasted_iota(jnp.int32, (LANES, tq), 0)
    z64 = jnp.zeros((FOX_HEAD_DIM, tq), BF16)

    def bias_rows(h):
        hit = (rows == h) | (rows == h + 8) | (rows == h + 16)
        return jnp.where(hit, -1.0, 0.0).astype(BF16)

    for p in range(N_PAIRS):
        lo = p * LANES
        qe = qT_ref[0, lo:lo + FOX_HEAD_DIM, :]
        qo = qT_ref[0, lo + FOX_HEAD_DIM:lo + LANES, :]
        wq_ref[p] = jnp.concatenate(
            [jnp.concatenate([qe, z64, bias_rows(2 * p)], axis=0),
             jnp.concatenate([z64, qo, bias_rows(2 * p + 1)], axis=0)], axis=1)
    m_ref[...] = jnp.full_like(m_ref, NEG_BIG)
    acc_ref[...] = jnp.zeros_like(acc_ref)

    kpos = lax.broadcasted_iota(jnp.int32, (tq, 2 * tq), 0)
    qpos = lax.broadcasted_iota(jnp.int32, (tq, 2 * tq), 1) % tq
    causal = kpos <= qpos

    def scores(p, k0):
        lo = p * LANES
        ka = jnp.concatenate([ktok_ref[0, pl.ds(k0, tq), lo:lo + LANES],
                              cp_ref[0, pl.ds(k0, tq), :]], axis=1)
        return _dot(ka, wq_ref[p])

    def colmax(p, s):
        m_prev = m_ref[p]
        m_new = jnp.maximum(m_prev, jnp.max(s, axis=0, keepdims=True))
        m_ref[p] = m_new
        return m_prev, m_new

    def expo(s, m_prev, m_new):
        return jnp.exp2(m_prev - m_new), jnp.exp2(s - m_new).astype(BF16)

    def weighted_v(p, k0, alpha, pb):
        for e in range(2):
            h = 2 * p + e
            va = vTb_ref[0, h * V_AUG:(h + 1) * V_AUG, pl.ds(k0, tq)]
            acc_ref[h] = alpha[:, e * tq:(e + 1) * tq] * acc_ref[h] + _dot(va, pb[:, e * tq:(e + 1) * tq])

    def trip(kj, last):
        k0 = pl.multiple_of(kj * tq, tq)
        k_prev = pl.multiple_of(jnp.maximum(kj - 1, 0) * tq, tq)
        s0 = s0_ref[...]
        if last:
            s0 = jnp.where(causal, s0, NEG_BIG)
        mask = (lambda s: jnp.where(causal, s, NEG_BIG)) if last else (lambda s: s)
        mm0 = colmax(0, s0)
        s1 = mask(scores(1, k0))
        e0 = expo(s0, *mm0)
        weighted_v(3, k_prev, alpha3_ref[...], pb3_ref[...])
        mm1 = colmax(1, s1)
        s2 = mask(scores(2, k0))
        e1 = expo(s1, *mm1)
        weighted_v(0, k0, *e0)
        mm2 = colmax(2, s2)
        s3 = mask(scores(3, k0))
        e2 = expo(s2, *mm2)
        weighted_v(1, k0, *e1)
        mm3 = colmax(3, s3)
        if not last:
            s0_ref[...] = scores(0, pl.multiple_of((kj + 1) * tq, tq))
        e3 = expo(s3, *mm3)
        weighted_v(2, k0, *e2)
        alpha3_ref[...], pb3_ref[...] = e3

    s0_ref[...] = scores(0, 0)
    alpha3_ref[...] = jnp.ones_like(alpha3_ref)
    pb3_ref[...] = jnp.zeros_like(pb3_ref)

    def body(kj, carry):
        trip(kj, False)
        return carry

    lax.fori_loop(0, qi, body, 0)
    trip(qi, True)
    weighted_v(3, pl.multiple_of(qi * tq, tq), alpha3_ref[...], pb3_ref[...])

    outs = []
    for h in range(FOX_HEADS):
        a = acc_ref[h]
        outs.append(a[0:FOX_HEAD_DIM] * (1.0 / a[FOX_HEAD_DIM:FOX_HEAD_DIM + 1]))
    oT = jnp.concatenate(outs, axis=0)
    o_ref[0] = (oT.T * fgs_ref[0].astype(F32)).astype(BF16)


def _fox(p, tq):
    B, _, L = p["qT"].shape
    assert L % tq == 0
    return pl.pallas_call(
        functools.partial(_fox_kernel, tq=tq),
        grid=(B, L // tq),
        in_specs=[pl.BlockSpec((1, FOX_WIDTH, tq), lambda b, i: (b, 0, i)),
                  pl.BlockSpec((1, L, FOX_WIDTH), lambda b, i: (b, 0, 0)),
                  pl.BlockSpec((1, L, LANES), lambda b, i: (b, 0, 0)),
                  pl.BlockSpec((1, FOX_HEADS * V_AUG, L), lambda b, i: (b, 0, 0)),
                  pl.BlockSpec((1, tq, FOX_WIDTH), lambda b, i: (b, i, 0))],
        out_specs=pl.BlockSpec((1, tq, FOX_WIDTH), lambda b, i: (b, i, 0)),
        out_shape=jax.ShapeDtypeStruct((B, L, FOX_WIDTH), BF16),
        scratch_shapes=[pltpu.VMEM((N_PAIRS, 2 * LANES, 2 * tq), BF16),
                        pltpu.VMEM((N_PAIRS, 1, 2 * tq), F32),
                        pltpu.VMEM((FOX_HEADS, V_AUG, tq), F32),
                        pltpu.VMEM((tq, 2 * tq), F32), pltpu.VMEM((1, 2 * tq), F32),
                        pltpu.VMEM((tq, 2 * tq), BF16)],
        compiler_params=pltpu.CompilerParams(
            dimension_semantics=("parallel", "arbitrary"), vmem_limit_bytes=VMEM_LIMIT),
        name="fox",
    )(p["qT"], p["ktok"], p["cp"], p["vTb"], p["fgs"])


GLA_C = 128
GLA_MID = GLA_C // 2 - 1


def _gla_consts():
    i = jnp.arange(GLA_C)
    incl = (i[:, None] <= i[None, :]).astype(F32)
    upto_mid = (i[:, None] <= GLA_MID).astype(F32)
    uc = incl - upto_mid
    ud = 1.0 - incl
    ones = jnp.ones((GLA_C, LANES), F32)
    wfeat = jnp.concatenate([uc, ud, ones], axis=1)
    wfeat3 = jnp.concatenate([wfeat] * 3, axis=0).astype(BF16)
    ltok = jnp.concatenate([uc.T, incl.T], axis=0)
    ltok3 = jnp.concatenate([ltok] * 3, axis=1).astype(BF16)
    return wfeat3, ltok3


def _gla_kernel(gq_ref, gkT_ref, gv_ref, la_ref, laT_ref, ggs_ref, gng_ref, wfeat_ref, ltok_ref,
                s0_ref, og_ref, s_ref):
    C = GLA_C

    @pl.when(pl.program_id(1) == 0)
    def _():
        s_ref[...] = s0_ref[...]

    g3 = jnp.concatenate(_split3(la_ref[0]), axis=0).astype(BF16)
    btok = _dot(ltok_ref[...], g3)
    bc, b = btok[0:C], btok[C:2 * C]
    gT3 = jnp.concatenate(_split3(laT_ref[0]), axis=1).astype(BF16)
    bfeat = _dot(gT3, wfeat_ref[...])
    bcT, dT, blast = bfeat[:, 0:C], bfeat[:, C:2 * C], bfeat[:, 2 * C:]

    gq = gq_ref[0].astype(F32)
    q_in = (gq * jnp.exp(bc)).astype(BF16)
    q_s = (gq * jnp.exp(b)).astype(BF16)
    gkT = gkT_ref[0].astype(F32)
    k_inT = (gkT * jnp.exp(-bcT)).astype(BF16)
    k_decT = (gkT * jnp.exp(dT)).astype(BF16)
    decay = jnp.exp(blast)

    ti = lax.broadcasted_iota(jnp.int32, (C, C), 0)
    si = lax.broadcasted_iota(jnp.int32, (C, C), 1)
    tril = si <= ti
    gng = gng_ref[...]
    for h in range(GLA_HEADS):
        ks = slice(h * GLA_DK, (h + 1) * GLA_DK)
        vs = slice(h * GLA_DV, (h + 1) * GLA_DV)
        att = jnp.where(tril, _dot(q_in[:, ks], k_inT[ks, :]), 0.0).astype(BF16)
        vh = gv_ref[0, :, vs]
        s_h = s_ref[0, h]
        o = _dot(att, vh) + _dot(q_s[:, ks], s_h.astype(BF16))
        s_ref[0, h] = jnp.concatenate([decay[ks]] * 2, axis=1) * s_h + _dot(k_decT[ks, :], vh)
        ms = jnp.mean(o * o, axis=-1, keepdims=True)
        on = (o * lax.rsqrt(ms + EPS)) * gng
        og_ref[0, :, vs] = (on * ggs_ref[0, :, vs].astype(F32)).astype(BF16)


def _gla(p, s0, gla_norm_g):
    B, L, _ = p["gq"].shape
    C = GLA_C
    assert L % C == 0
    wfeat3, ltok3 = _gla_consts()
    tok = lambda w: pl.BlockSpec((1, C, w), lambda b, i: (b, i, 0))
    feat = lambda r: pl.BlockSpec((1, r, C), lambda b, i: (b, 0, i))
    st = pl.BlockSpec((1, GLA_HEADS, GLA_DK, GLA_DV), lambda b, i: (b, 0, 0, 0))
    return pl.pallas_call(
        _gla_kernel,
        grid=(B, L // C),
        in_specs=[tok(GLA_KW), feat(GLA_KW), tok(GLA_VW), tok(GLA_KW), feat(GLA_KW), tok(GLA_VW),
                  _const_spec((1, GLA_DV)), _const_spec(wfeat3.shape), _const_spec(ltok3.shape), st],
        out_specs=[tok(GLA_VW), st],
        out_shape=[jax.ShapeDtypeStruct((B, L, GLA_VW), BF16),
                   jax.ShapeDtypeStruct((B, GLA_HEADS, GLA_DK, GLA_DV), F32)],
        compiler_params=pltpu.CompilerParams(
            dimension_semantics=("parallel", "arbitrary"), vmem_limit_bytes=VMEM_LIMIT),
        name="gla",
    )(p["gq"], p["gkT"], p["gv"], p["la"], p["laT"], p["ggs"], gla_norm_g[None, :], wfeat3, ltok3, s0)


def _merge_kernel(x_ref, of_ref, og_ref, sa_ref, sb_ref, wa_ref, wb_ref, wo_ref, y_ref):
    a = _dot(of_ref[0], wa_ref[...])
    b = _dot(og_ref[0], wb_ref[...])
    m = sa_ref[0].astype(F32) * a + sb_ref[0].astype(F32) * b
    y_ref[0] = x_ref[0] + _dot(m.astype(BF16), wo_ref[...])


def _merge(x, of, og, sa, sb, w_up_a, w_up_b, w_out, tm):
    B, L, D = x.shape
    assert L % tm == 0
    tok = lambda w: pl.BlockSpec((1, tm, w), lambda b, i: (b, i, 0))
    return pl.pallas_call(
        _merge_kernel,
        grid=(B, L // tm),
        in_specs=[tok(D), tok(FOX_WIDTH), tok(GLA_VW), tok(D), tok(D),
                  _const_spec((FOX_WIDTH, D)), _const_spec((GLA_VW, D)), _const_spec((D, D))],
        out_specs=tok(D),
        out_shape=jax.ShapeDtypeStruct((B, L, D), F32),
        compiler_params=pltpu.CompilerParams(
            dimension_semantics=("parallel", "parallel"), vmem_limit_bytes=VMEM_LIMIT),
        name="merge",
    )(x, of, og, sa, sb, w_up_a.astype(BF16), w_up_b.astype(BF16), w_out.astype(BF16))


def _prompt_layer(x, wts, gla_norm_g, w_up_a, w_up_b, w_out):
    B, L, _ = x.shape
    p = _proj(x, wts, 256)
    of = _fox(p, 256)
    s0 = jnp.zeros((B, GLA_HEADS, GLA_DK, GLA_DV), F32)
    og, s_end = _gla(p, s0, gla_norm_g)
    y = _merge(x, of, og, p["sa"], p["sb"], w_up_a, w_up_b, w_out, 512)
    k_out = p["kT"].reshape(B, FOX_HEADS, FOX_HEAD_DIM, L).transpose(0, 3, 1, 2)
    v_out = p["vT"].reshape(B, FOX_HEADS, FOX_HEAD_DIM, L).transpose(0, 3, 1, 2)
    lf_out = p["lfT"].transpose(0, 2, 1)
    return y, k_out, v_out, lf_out, s_end


DEC_PP = 16


def _fox_dec_kernel(pt_ref, *refs, T):
    PP = DEC_PP
    k_refs, v_refs, lf_refs = refs[0:PP], refs[PP:2 * PP], refs[2 * PP:3 * PP]
    qbd_ref, knT_ref, vn_ref, lfn_ref, fgs_ref = refs[3 * PP:3 * PP + 5]
    o_ref = refs[3 * PP + 5]
    m_ref, l_ref, acc_ref, carry_ref = refs[3 * PP + 6:]
    j = pl.program_id(1)
    H, P, HD = FOX_HEADS, PAGE_SIZE, FOX_HEAD_DIM
    R = H * T

    @pl.when(j == 0)
    def _():
        m_ref[...] = jnp.full_like(m_ref, NEG_BIG)
        l_ref[...] = jnp.zeros_like(l_ref)
        acc_ref[...] = jnp.zeros_like(acc_ref)
        carry_ref[...] = jnp.zeros_like(carry_ref)

    lf = jnp.concatenate([r[0] for r in lf_refs], axis=0) * LOG2E
    pieces = jnp.concatenate(_split3(lf), axis=0).astype(BF16)
    ri = lax.broadcasted_iota(jnp.int32, (P, 2 * P), 0)
    ci = lax.broadcasted_iota(jnp.int32, (P, 2 * P), 1)
    wcum = jnp.where((ri <= ci) | (ci >= P), 1.0, 0.0).astype(BF16)
    cw3 = _dot(pieces, wcum)
    n = PP * H
    cw = cw3[0:n] + cw3[n:2 * n] + cw3[2 * n:3 * n]
    off = carry_ref[...]
    bias = []
    for i in range(PP):
        c_i = cw[i * H:(i + 1) * H, 0:P] + off
        bias.append(jnp.broadcast_to(c_i[:, None, :], (H, T, P)).reshape(R, P))
        off = off + cw[i * H:(i + 1) * H, P:2 * P]
    carry_ref[...] = off

    def online(s, v_dot):
        m_prev = m_ref[:, 0:1]
        m_new = jnp.maximum(m_prev, jnp.max(s, axis=-1, keepdims=True))
        alpha = jnp.exp2(m_prev - m_new)
        p = jnp.exp2(s - m_new)
        l_new = alpha * l_ref[:, 0:1] + jnp.sum(p, axis=-1, keepdims=True)
        acc_ref[...] = alpha * acc_ref[...] + v_dot(p.astype(BF16))
        m_ref[...] = jnp.broadcast_to(m_new, (R, LANES))
        l_ref[...] = jnp.broadcast_to(l_new, (R, LANES))

    qbd = qbd_ref[0]
    k_all = jnp.concatenate([r[0].reshape(H * HD, P) for r in k_refs], axis=1).astype(BF16)
    v_all = jnp.concatenate([r[0].reshape(H * HD, P) for r in v_refs], axis=1).astype(BF16)
    s = _dot(qbd, k_all) - jnp.concatenate(bias, axis=1)
    online(s, lambda pb: _dot_nt(pb, v_all))

    @pl.when(j == pl.num_programs(1) - 1)
    def _():
        lfn = lfn_ref[0] * LOG2E
        pn = jnp.concatenate(_split3(lfn), axis=0).astype(BF16)
        r8 = lax.broadcasted_iota(jnp.int32, (T, T), 0)
        c8 = lax.broadcasted_iota(jnp.int32, (T, T), 1)
        u8 = jnp.where(r8 <= c8, 1.0, 0.0).astype(BF16)
        cn3 = _dot(pn, u8)
        c_new = carry_ref[:, 0:T] + cn3[0:H] + cn3[H:2 * H] + cn3[2 * H:3 * H]
        bias_n = jnp.broadcast_to(c_new[:, None, :], (H, T, T)).reshape(R, T)
        s_n = _dot(qbd, knT_ref[0].astype(BF16)) - bias_n
        t_row = lax.broadcasted_iota(jnp.int32, (R, T), 0) % T
        t_col = lax.broadcasted_iota(jnp.int32, (R, T), 1)
        s_n = jnp.where(t_col <= t_row, s_n, NEG_BIG)
        vn = vn_ref[0].astype(BF16)
        online(s_n, lambda pb: _dot(pb, vn))
        res = acc_ref[...] / l_ref[:, 0:1]
        lane_head = lax.broadcasted_iota(jnp.int32, (T, H * HD), 1) // HD
        out = jnp.zeros((T, H * HD), F32)
        for h in range(H):
            out = out + jnp.where(lane_head == h, res[h * T:(h + 1) * T], 0.0)
        o_ref[0] = out * fgs_ref[0]


def _fox_dec(page_table, ck, cv, clf, qbd, knT, vn, lfn, fgs3):
    Bd, n_pages = page_table.shape
    PP = DEC_PP
    assert n_pages % PP == 0
    T = vn.shape[1]
    H, HD, P = FOX_HEADS, FOX_HEAD_DIM, PAGE_SIZE
    R = H * T

    def page4(i):
        return pl.BlockSpec((1, H, HD, P), lambda b, j, pt: (pt[b, j * PP + i], 0, 0, 0))

    def page3(i):
        return pl.BlockSpec((1, H, P), lambda b, j, pt: (pt[b, j * PP + i], 0, 0))

    per_b = lambda s: pl.BlockSpec((1,) + s, lambda b, j, pt: (b, 0, 0))
    in_specs = ([page4(i) for i in range(PP)] + [page4(i) for i in range(PP)] + [page3(i) for i in range(PP)]
                + [per_b((R, FOX_WIDTH)), per_b((FOX_WIDTH, T)), per_b((T, FOX_WIDTH)), per_b((H, T)),
                   per_b((T, FOX_WIDTH))])
    grid_spec = pltpu.PrefetchScalarGridSpec(
        num_scalar_prefetch=1,
        grid=(Bd, n_pages // PP),
        in_specs=in_specs,
        out_specs=per_b((T, FOX_WIDTH)),
        scratch_shapes=[pltpu.VMEM((R, LANES), F32), pltpu.VMEM((R, LANES), F32),
                        pltpu.VMEM((R, FOX_WIDTH), F32), pltpu.VMEM((H, LANES), F32)])
    return pl.pallas_call(
        functools.partial(_fox_dec_kernel, T=T),
        grid_spec=grid_spec,
        out_shape=jax.ShapeDtypeStruct((Bd, T, FOX_WIDTH), F32),
        compiler_params=pltpu.CompilerParams(
            dimension_semantics=("parallel", "arbitrary"), vmem_limit_bytes=VMEM_LIMIT),
        name="fox_dec",
    )(page_table, *([ck] * PP), *([cv] * PP), *([clf] * PP), qbd, knT, vn, lfn, fgs3)


def _gla_dec_kernel(gq_ref, gkT_ref, gv_ref, la_ref, laT_ref, ggs_ref, gng_ref, s0_ref, og_ref, s_ref, *, T):
    r8 = lax.broadcasted_iota(jnp.int32, (T, T), 0)
    c8 = lax.broadcasted_iota(jnp.int32, (T, T), 1)
    low = jnp.where(c8 <= r8, 1.0, 0.0).astype(BF16)
    g3 = jnp.concatenate(_split3(la_ref[0]), axis=0).astype(BF16)
    b = _dot(jnp.concatenate([low] * 3, axis=1), g3)
    gT3 = jnp.concatenate(_split3(laT_ref[0]), axis=1).astype(BF16)
    ri = lax.broadcasted_iota(jnp.int32, (T, T + LANES), 0)
    ci = lax.broadcasted_iota(jnp.int32, (T, T + LANES), 1)
    wf = jnp.where((ri <= ci) | (ci >= T), 1.0, 0.0).astype(BF16)
    bf = _dot(gT3, jnp.concatenate([wf] * 3, axis=0))
    bT, blast = bf[:, 0:T], bf[:, T:]
    q_in = (gq_ref[0] * jnp.exp(b)).astype(BF16)
    gkT = gkT_ref[0]
    k_inT = (gkT * jnp.exp(-bT)).astype(BF16)
    k_decT = (gkT * jnp.exp(blast[:, 0:T] - bT)).astype(BF16)
    decay = jnp.exp(blast)
    gng = gng_ref[...]
    for h in range(GLA_HEADS):
        ks = slice(h * GLA_DK, (h + 1) * GLA_DK)
        vs = slice(h * GLA_DV, (h + 1) * GLA_DV)
        att = jnp.where(c8 <= r8, _dot(q_in[:, ks], k_inT[ks, :]), 0.0).astype(BF16)
        vh = gv_ref[0, :, vs].astype(BF16)
        s_h = s0_ref[0, h]
        o = _dot(att, vh) + _dot(q_in[:, ks], s_h.astype(BF16))
        s_ref[0, h] = jnp.concatenate([decay[ks]] * 2, axis=1) * s_h + _dot(k_decT[ks, :], vh)
        ms = jnp.mean(o * o, axis=-1, keepdims=True)
        og_ref[0, :, vs] = ((o * lax.rsqrt(ms + EPS)) * gng) * ggs_ref[0, :, vs]


def _gla_dec(gq, gkT, gv, la, laT, ggs, gla_norm_g, s0):
    Bd, T, _ = gq.shape
    tok = lambda w: pl.BlockSpec((1, T, w), lambda b: (b, 0, 0))
    feat = lambda r: pl.BlockSpec((1, r, T), lambda b: (b, 0, 0))
    st = pl.BlockSpec((1, GLA_HEADS, GLA_DK, GLA_DV), lambda b: (b, 0, 0, 0))
    return pl.pallas_call(
        functools.partial(_gla_dec_kernel, T=T),
        grid=(Bd,),
        in_specs=[tok(GLA_KW), feat(GLA_KW), tok(GLA_VW), tok(GLA_KW), feat(GLA_KW), tok(GLA_VW),
                  _const_spec((1, GLA_DV)), st],
        out_specs=[tok(GLA_VW), st],
        out_shape=[jax.ShapeDtypeStruct((Bd, T, GLA_VW), F32),
                   jax.ShapeDtypeStruct((Bd, GLA_HEADS, GLA_DK, GLA_DV), F32)],
        compiler_params=pltpu.CompilerParams(
            dimension_semantics=("parallel",), vmem_limit_bytes=VMEM_LIMIT),
        name="gla_dec",
    )(gq, gkT, gv, la, laT, ggs, gla_norm_g[None, :], s0)


def _sample_layer(x, wts, cache_k, cache_v, cache_logf, state, page_table, gla_norm_g, w_up_a, w_up_b, w_out):
    Bd, T, D = x.shape
    N = Bd * T
    H, HD = FOX_HEADS, FOX_HEAD_DIM
    p = _proj(x.reshape(1, N, D), wts, N)
    feat_bt = lambda a, r: a[0].astype(F32).reshape(r, Bd, T)
    tok_bt = lambda a: a[0].astype(F32).reshape(Bd, T, -1)
    qf = feat_bt(p["qT"], FOX_WIDTH).reshape(H, HD, Bd, T)
    kf = feat_bt(p["kT"], FOX_WIDTH).reshape(H, HD, Bd, T)
    vf = feat_bt(p["vT"], FOX_WIDTH).reshape(H, HD, Bd, T)
    lff = feat_bt(p["lfT"], H)
    ck = jnp.transpose(cache_k, (0, 2, 3, 1))
    cv = jnp.transpose(cache_v, (0, 2, 3, 1))
    clf = jnp.transpose(cache_logf, (0, 2, 1))
    q_bhtd = qf.transpose(2, 0, 3, 1)
    qbd = (q_bhtd[:, :, :, None, :] * jnp.eye(H, dtype=F32)[None, :, None, :, None]
           ).reshape(Bd, H * T, FOX_WIDTH).astype(BF16)
    of = _fox_dec(page_table, ck, cv, clf, qbd,
                  kf.reshape(FOX_WIDTH, Bd, T).transpose(1, 0, 2),
                  vf.reshape(FOX_WIDTH, Bd, T).transpose(1, 2, 0),
                  lff.transpose(1, 0, 2), tok_bt(p["fgs"]))
    og, s_end = _gla_dec(tok_bt(p["gq"]), feat_bt(p["gkT"], GLA_KW).transpose(1, 0, 2), tok_bt(p["gv"]),
                         tok_bt(p["la"]), feat_bt(p["laT"], GLA_KW).transpose(1, 0, 2), tok_bt(p["ggs"]),
                         gla_norm_g, state)
    y = _merge(x.reshape(1, N, D), of.reshape(1, N, FOX_WIDTH).astype(BF16), og.reshape(1, N, GLA_VW).astype(BF16),
               p["sa"], p["sb"], w_up_a, w_up_b, w_out, N)
    k_out = kf.transpose(2, 3, 0, 1)
    v_out = vf.transpose(2, 3, 0, 1)
    lf_out = lff.transpose(1, 2, 0)
    return y.reshape(Bd, T, D), k_out, v_out, lf_out, s_end


def kernel(x_prompt, x_sample, cache_k, cache_v, cache_logf, state_gla, page_table, ln_g, w_in, fox_b_f, q_norm_g, k_norm_g, gla_w_a2, gla_b_a, gla_norm_g, w_up_a, w_up_b, w_out):
    wts = _prep_weights(ln_g[0], w_in[0], fox_b_f[0], q_norm_g[0], k_norm_g[0], gla_w_a2[0], gla_b_a[0])
    yp, kp, vp, lfp, sp = _prompt_layer(x_prompt, wts, gla_norm_g[0], w_up_a[0], w_up_b[0], w_out[0])
    ys, kq, vq, lfq, sq = _sample_layer(x_sample, wts, cache_k[0], cache_v[0], cache_logf[0], state_gla[0],
                                        page_table, gla_norm_g[0], w_up_a[0], w_up_b[0], w_out[0])
    return (yp, ys, kp[None], vp[None], lfp[None], sp[None], kq[None], vq[None], lfq[None], sq[None])
```

```python
import functools

import jax
import jax.numpy as jnp
from jax import lax
from jax.experimental import pallas as pl
from jax.experimental.pallas import tpu as pltpu

F32 = jnp.float32
BF16 = jnp.bfloat16

D_MODEL = 1024
FOX_HEADS = 8
FOX_HEAD_DIM = 64
FOX_WIDTH = FOX_HEADS * FOX_HEAD_DIM
FOX_SCALE = FOX_HEAD_DIM ** -0.5
LOG2E = 1.4426950408889634
V_AUG = FOX_HEAD_DIM + 16
GLA_HEADS = 4
GLA_DK = 128
GLA_DV = 256
GLA_KW = GLA_HEADS * GLA_DK
GLA_VW = GLA_HEADS * GLA_DV
GLA_RANK = 16
GLA_TAU = 16.0
EPS = 1e-6
PAGE_SIZE = 128

LANES = 128
SUBLANES = 8
VMEM_LIMIT = 48 * 1024 * 1024
VMEM_LIMIT_FUSED = 60000 * 1024
PROJ_TM = 256
FOX_TQ = 256
MERGE_TM = 512
GLA_C = 128
DEC_PP = 16

_SIZES = (FOX_WIDTH, FOX_WIDTH, FOX_WIDTH, FOX_HEADS, FOX_WIDTH,
          GLA_KW, GLA_KW, GLA_VW, GLA_RANK, GLA_VW, D_MODEL, D_MODEL)
_OFF = [0]
for _s in _SIZES:
    _OFF.append(_OFF[-1] + _s)
(_FQ, _FK, _FV, _FF, _FG, _GQ, _GK, _GV, _GLR, _GG, _MA, _MB) = _OFF[:-1]

_WF_SMALL = 4 * FOX_WIDTH
_WF_ROWS = _WF_SMALL + 32
_WT_COLS = FOX_WIDTH + GLA_KW + GLA_VW + GLA_VW + 2 * D_MODEL

NEG_BIG = -1e30
N_PAIRS = FOX_HEADS // 2


def _dot_nt(a, b):
    return lax.dot_general(a, b, (((1,), (1,)), ((), ())), preferred_element_type=F32)


def _dot(a, b):
    return jnp.dot(a, b, preferred_element_type=F32)


def _log_sigmoid(x):
    return -(jnp.maximum(-x, 0.0) + jnp.log1p(jnp.exp(-jnp.abs(x))))


def _split3(a):
    hi = a.astype(BF16).astype(F32)
    r = a - hi
    mid = r.astype(BF16).astype(F32)
    lo = (r - mid).astype(BF16).astype(F32)
    return hi, mid, lo


def _lane_tile(a, n):
    return a if n == 1 else jnp.concatenate([a] * n, axis=1)


def _const_spec(shape):
    nd = len(shape)
    return pl.BlockSpec(shape, lambda *_: (0,) * nd, pipeline_mode=pl.Buffered(1))


_PROJ_OUTS = ("qT", "kT", "ktok", "cp", "vT", "vTb", "lfT", "fgs", "gq", "gkT", "gv", "la", "laT", "ggs", "sa", "sb")
_N_PROJ_IN = 9


def _proj_stages(in_refs, out_refs, carry_ref, h_ref, first_tile, tm):
    x_ref, lng_ref, wf_ref, wt_ref, qg_ref, kg_ref, fb_ref, wa2t_ref, bat_ref = in_refs
    o = dict(zip(_PROJ_OUTS, out_refs))
    nrep = tm // LANES

    def headnorm(t, g_ref):
        outs = []
        for hh in range(FOX_HEADS):
            blk = t[hh * FOX_HEAD_DIM:(hh + 1) * FOX_HEAD_DIM]
            ssq = jnp.sum(blk * blk, axis=0, keepdims=True) * (1.0 / FOX_HEAD_DIM)
            g = _lane_tile(g_ref[hh * FOX_HEAD_DIM:(hh + 1) * FOX_HEAD_DIM], nrep)
            outs.append((blk * lax.rsqrt(ssq + EPS)) * g)
        return jnp.concatenate(outs, axis=0)

    def tok_group(c0, n, fn, ref):
        for j in range(n // 512):
            z = _dot(h_ref[...], wt_ref[:, c0 + j * 512:c0 + (j + 1) * 512])
            ref[0, :, j * 512:(j + 1) * 512] = fn(z).astype(BF16)

    silu = lambda z: z * jax.nn.sigmoid(z)

    def stage_a():
        @pl.when(first_tile)
        def _():
            carry_ref[...] = jnp.zeros_like(carry_ref)

        x = x_ref[0]
        ms = jnp.mean(x * x, axis=-1, keepdims=True)
        h_ref[...] = ((x * lax.rsqrt(ms + EPS)) * lng_ref[...]).astype(BF16)
        h = h_ref[...]
        q = headnorm(_dot_nt(wf_ref[0:FOX_WIDTH], h), qg_ref)
        o["qT"][0] = (q * (FOX_SCALE * LOG2E)).astype(BF16)
        k = headnorm(_dot_nt(wf_ref[FOX_WIDTH:2 * FOX_WIDTH], h), kg_ref)
        o["kT"][0] = k
        o["ktok"][0] = k.T.astype(BF16)
        v = _dot_nt(wf_ref[2 * FOX_WIDTH:3 * FOX_WIDTH], h)
        o["vT"][0] = v
        ones_rows = jnp.where(lax.broadcasted_iota(jnp.int32, (V_AUG - FOX_HEAD_DIM, tm), 0) == 0, 1.0, 0.0)
        vaug = []
        for hh in range(FOX_HEADS):
            vaug += [v[hh * FOX_HEAD_DIM:(hh + 1) * FOX_HEAD_DIM], ones_rows]
        o["vTb"][0] = jnp.concatenate(vaug, axis=0).astype(BF16)

    def stage_b():
        h = h_ref[...]
        o["gkT"][0] = _dot_nt(wf_ref[3 * FOX_WIDTH:4 * FOX_WIDTH], h).astype(BF16)
        small = _dot_nt(wf_ref[_WF_SMALL:_WF_ROWS], h)
        lf = _log_sigmoid(small[0:FOX_HEADS] + _lane_tile(fb_ref[...], nrep))
        o["lfT"][0] = lf
        pieces = jnp.concatenate(_split3(lf), axis=0).astype(BF16)
        ri = lax.broadcasted_iota(jnp.int32, (tm, tm), 0)
        ci = lax.broadcasted_iota(jnp.int32, (tm, tm), 1)
        utri = jnp.where(ri <= ci, 1.0, 0.0).astype(BF16)
        cum3 = _dot(pieces, utri)
        tot3 = _dot(pieces, jnp.ones((tm, LANES), BF16))
        cum = cum3[0:8] + cum3[8:16] + cum3[16:24]
        tot = tot3[0:8] + tot3[8:16] + tot3[16:24]
        c = cum + _lane_tile(carry_ref[...], nrep)
        carry_ref[...] = carry_ref[...] + tot
        cpieces = jnp.concatenate(list(_split3(c * LOG2E)) + [jnp.zeros((LANES - 24, tm), F32)], axis=0)
        o["cp"][0] = cpieces.T.astype(BF16)
        glr = small[FOX_HEADS:FOX_HEADS + GLA_RANK].astype(BF16)
        pre = _dot(wa2t_ref[...], glr) + _lane_tile(bat_ref[...], nrep)
        la = _log_sigmoid(pre) * (1.0 / GLA_TAU)
        o["laT"][0] = la
        o["la"][0] = la.T
        tok_group(0, FOX_WIDTH, silu, o["fgs"])

    def stage_c():
        tok_group(FOX_WIDTH, GLA_KW, lambda z: z * (GLA_DK ** -0.5), o["gq"])
        tok_group(FOX_WIDTH + GLA_KW, GLA_VW, lambda z: z, o["gv"])
        tok_group(FOX_WIDTH + GLA_KW + GLA_VW, GLA_VW, silu, o["ggs"])

    def stage_d():
        tok_group(FOX_WIDTH + GLA_KW + 2 * GLA_VW, D_MODEL, jax.nn.sigmoid, o["sa"])
        tok_group(FOX_WIDTH + GLA_KW + 2 * GLA_VW + D_MODEL, D_MODEL, jax.nn.sigmoid, o["sb"])

    return stage_a, stage_b, stage_c, stage_d


def _proj_kernel(*refs, tm):
    in_refs = refs[:_N_PROJ_IN]
    out_refs = refs[_N_PROJ_IN:_N_PROJ_IN + len(_PROJ_OUTS)]
    carry_ref, h_ref = refs[_N_PROJ_IN + len(_PROJ_OUTS):]
    for stage in _proj_stages(in_refs, out_refs, carry_ref, h_ref, pl.program_id(1) == 0, tm):
        stage()


def _proj_out_specs(B, L, tm, imap_tok, imap_feat):
    tok = lambda w, dt: (jax.ShapeDtypeStruct((B, L, w), dt), pl.BlockSpec((1, tm, w), imap_tok))
    feat = lambda r, dt: (jax.ShapeDtypeStruct((B, r, L), dt), pl.BlockSpec((1, r, tm), imap_feat))
    outs = dict(
        qT=feat(FOX_WIDTH, BF16), kT=feat(FOX_WIDTH, F32), ktok=tok(FOX_WIDTH, BF16), cp=tok(LANES, BF16),
        vT=feat(FOX_WIDTH, F32), vTb=feat(FOX_HEADS * V_AUG, BF16), lfT=feat(FOX_HEADS, F32), fgs=tok(FOX_WIDTH, BF16),
        gq=tok(GLA_KW, BF16), gkT=feat(GLA_KW, BF16), gv=tok(GLA_VW, BF16), la=tok(GLA_KW, F32),
        laT=feat(GLA_KW, F32), ggs=tok(GLA_VW, BF16), sa=tok(D_MODEL, BF16), sb=tok(D_MODEL, BF16))
    assert tuple(outs) == _PROJ_OUTS
    return [outs[n][0] for n in _PROJ_OUTS], [outs[n][1] for n in _PROJ_OUTS]


def _proj_in_specs(tm, imap_tok):
    D = D_MODEL
    return [pl.BlockSpec((1, tm, D), imap_tok),
            _const_spec((1, D)), _const_spec((_WF_ROWS, D)), _const_spec((D, _WT_COLS)),
            _const_spec((FOX_WIDTH, LANES)), _const_spec((FOX_WIDTH, LANES)),
            _const_spec((FOX_HEADS, LANES)), _const_spec((GLA_KW, GLA_RANK)),
            _const_spec((GLA_KW, LANES))]


def _proj_args(x, wts):
    return (x, wts["ln_g"], wts["wf"], wts["wt"], wts["qg"], wts["kg"], wts["fb"], wts["wa2t"], wts["bat"])


def _proj(x, wts, tm):
    B, L, D = x.shape
    assert L % tm == 0 and tm % LANES == 0
    out_shape, out_specs = _proj_out_specs(B, L, tm, lambda b, i: (b, i, 0), lambda b, i: (b, 0, i))
    res = pl.pallas_call(
        functools.partial(_proj_kernel, tm=tm),
        grid=(B, L // tm),
        in_specs=_proj_in_specs(tm, lambda b, i: (b, i, 0)),
        out_specs=out_specs,
        out_shape=out_shape,
        scratch_shapes=[pltpu.VMEM((FOX_HEADS, LANES), F32), pltpu.VMEM((tm, D), BF16)],
        compiler_params=pltpu.CompilerParams(
            dimension_semantics=("parallel", "arbitrary"), vmem_limit_bytes=VMEM_LIMIT),
        name="proj",
    )(*_proj_args(x, wts))
    return dict(zip(_PROJ_OUTS, res))


def _prep_weights(ln_g, w_in, fox_b_f, q_norm_g, k_norm_g, gla_w_a2, gla_b_a):
    w = w_in
    sl = lambda o, n: w[:, o:o + n]
    small = jnp.concatenate([sl(_FF, FOX_HEADS), sl(_GLR, GLA_RANK), jnp.zeros((D_MODEL, 8), F32)], axis=1)
    wf = jnp.concatenate([sl(_FQ, FOX_WIDTH), sl(_FK, FOX_WIDTH), sl(_FV, FOX_WIDTH), sl(_GK, GLA_KW), small],
                         axis=1).T.astype(BF16)
    wt = jnp.concatenate([sl(_FG, FOX_WIDTH), sl(_GQ, GLA_KW), sl(_GV, GLA_VW), sl(_GG, GLA_VW),
                          sl(_MA, D_MODEL), sl(_MB, D_MODEL)], axis=1).astype(BF16)
    rep = lambda vec: jnp.broadcast_to(vec[:, None], (vec.shape[0], LANES)).astype(F32)
    return dict(ln_g=ln_g[None, :], wf=wf, wt=wt,
                qg=rep(jnp.tile(q_norm_g, FOX_HEADS)), kg=rep(jnp.tile(k_norm_g, FOX_HEADS)),
                fb=rep(fox_b_f), wa2t=gla_w_a2.T.astype(BF16), bat=rep(gla_b_a))


def _dec_substep(kbuf, vbuf, lbuf, qbd_ref, knT_ref, vn_ref, lfn_ref, fgs_ref, o_ref,
                 m_ref, l_ref, acc_ref, carry_ref, first, last, T):
    PP = DEC_PP
    H, P, HD = FOX_HEADS, PAGE_SIZE, FOX_HEAD_DIM
    R = H * T

    @pl.when(first)
    def _():
        m_ref[...] = jnp.full_like(m_ref, NEG_BIG)
        l_ref[...] = jnp.zeros_like(l_ref)
        acc_ref[...] = jnp.zeros_like(acc_ref)
        carry_ref[...] = jnp.zeros_like(carry_ref)

    lf = jnp.concatenate([lbuf[i] for i in range(PP)], axis=0) * LOG2E
    pieces = jnp.concatenate(_split3(lf), axis=0).astype(BF16)
    ri = lax.broadcasted_iota(jnp.int32, (P, 2 * P), 0)
    ci = lax.broadcasted_iota(jnp.int32, (P, 2 * P), 1)
    wcum = jnp.where((ri <= ci) | (ci >= P), 1.0, 0.0).astype(BF16)
    cw3 = _dot(pieces, wcum)
    n = PP * H
    cw = cw3[0:n] + cw3[n:2 * n] + cw3[2 * n:3 * n]
    off = carry_ref[...]
    bias = []
    for i in range(PP):
        c_i = cw[i * H:(i + 1) * H, 0:P] + off
        bias.append(jnp.broadcast_to(c_i[:, None, :], (H, T, P)).reshape(R, P))
        off = off + cw[i * H:(i + 1) * H, P:2 * P]
    carry_ref[...] = off

    def online(s, v_dot):
        m_prev = m_ref[:, 0:1]
        m_new = jnp.maximum(m_prev, jnp.max(s, axis=-1, keepdims=True))
        alpha = jnp.exp2(m_prev - m_new)
        p = jnp.exp2(s - m_new)
        l_new = alpha * l_ref[:, 0:1] + jnp.sum(p, axis=-1, keepdims=True)
        acc_ref[...] = alpha * acc_ref[...] + v_dot(p.astype(BF16))
        m_ref[...] = jnp.broadcast_to(m_new, (R, LANES))
        l_ref[...] = jnp.broadcast_to(l_new, (R, LANES))

    qbd = qbd_ref[0]
    k_all = jnp.concatenate([kbuf[i].reshape(H * HD, P) for i in range(PP)], axis=1).astype(BF16)
    s = _dot(qbd, k_all) - jnp.concatenate(bias, axis=1)
    v_all = jnp.concatenate([vbuf[i].reshape(H * HD, P) for i in range(PP)], axis=1).astype(BF16)
    online(s, lambda pb: _dot_nt(pb, v_all))

    @pl.when(last)
    def _():
        lfn = lfn_ref[0] * LOG2E
        pn = jnp.concatenate(_split3(lfn), axis=0).astype(BF16)
        r8 = lax.broadcasted_iota(jnp.int32, (T, T), 0)
        c8 = lax.broadcasted_iota(jnp.int32, (T, T), 1)
        u8 = jnp.where(r8 <= c8, 1.0, 0.0).astype(BF16)
        cn3 = _dot(pn, u8)
        c_new = carry_ref[:, 0:T] + cn3[0:H] + cn3[H:2 * H] + cn3[2 * H:3 * H]
        bias_n = jnp.broadcast_to(c_new[:, None, :], (H, T, T)).reshape(R, T)
        s_n = _dot(qbd, knT_ref[0].astype(BF16)) - bias_n
        t_row = lax.broadcasted_iota(jnp.int32, (R, T), 0) % T
        t_col = lax.broadcasted_iota(jnp.int32, (R, T), 1)
        s_n = jnp.where(t_col <= t_row, s_n, NEG_BIG)
        vn = vn_ref[0].astype(BF16)
        online(s_n, lambda pb: _dot(pb, vn))
        res = acc_ref[...] / l_ref[:, 0:1]
        lane_head = lax.broadcasted_iota(jnp.int32, (T, H * HD), 1) // HD
        out = jnp.zeros((T, H * HD), F32)
        for h in range(H):
            out = out + jnp.where(lane_head == h, res[h * T:(h + 1) * T], 0.0)
        o_ref[0] = out * fgs_ref[0]


_N_DEC_IN = 8


def _proj_dec_kernel(pt_ref, *refs, tm, tiles_per_seq, sub_per_step, sub_per_b, T):
    n_in = _N_PROJ_IN + _N_DEC_IN
    proj_in = refs[:_N_PROJ_IN]
    ck_hbm, cv_hbm, clf_hbm, qbd_ref, knT_ref, vn_ref, lfn_ref, fgs_ref = refs[_N_PROJ_IN:n_in]
    proj_out = refs[n_in:n_in + len(_PROJ_OUTS)]
    of_ref = refs[n_in + len(_PROJ_OUTS)]
    (carry_ref, h_ref, kbuf, vbuf, lbuf, sem, m_ref, l_ref, acc_ref, dcarry_ref) = refs[n_in + len(_PROJ_OUTS) + 1:]
    g = pl.program_id(0)
    n_sub = pl.num_programs(0) * sub_per_step
    PP = DEC_PP

    def page_copies(s, slot):
        bd = s // sub_per_b
        j = s % sub_per_b
        out = []
        for i in range(PP):
            pg = pt_ref[bd, j * PP + i]
            out.append(pltpu.make_async_copy(ck_hbm.at[pg], kbuf.at[slot, i], sem.at[0, slot]))
            out.append(pltpu.make_async_copy(cv_hbm.at[pg], vbuf.at[slot, i], sem.at[1, slot]))
            out.append(pltpu.make_async_copy(clf_hbm.at[pg], lbuf.at[slot, i], sem.at[2, slot]))
        return out

    @pl.when(g == 0)
    def _():
        for c in page_copies(0, 0):
            c.start()

    stages = _proj_stages(proj_in, proj_out, carry_ref, h_ref, g % tiles_per_seq == 0, tm)
    assert sub_per_step == len(stages) and sub_per_step % 2 == 0
    for sub in range(sub_per_step):
        s = g * sub_per_step + sub
        slot = sub % 2

        for c in page_copies(s, slot):
            c.wait()
        if sub + 1 < sub_per_step:
            for c in page_copies(s + 1, 1 - slot):
                c.start()
        else:
            @pl.when(s + 1 < n_sub)
            def _():
                for c in page_copies(s + 1, 1 - slot):
                    c.start()
        j = s % sub_per_b
        _dec_substep(kbuf.at[slot], vbuf.at[slot], lbuf.at[slot], qbd_ref, knT_ref, vn_ref, lfn_ref, fgs_ref,
                     of_ref, m_ref, l_ref, acc_ref, dcarry_ref, j == 0, j == sub_per_b - 1, T)
        stages[sub]()


def _proj_dec(x, wts, page_table, ck, cv, clf, qbd, knT, vn, lfn, fgs3):
    B, L, D = x.shape
    tm = PROJ_TM
    Bd, n_pages = page_table.shape
    T = vn.shape[1]
    H, HD, P, PP = FOX_HEADS, FOX_HEAD_DIM, PAGE_SIZE, DEC_PP
    R = H * T
    assert L % tm == 0 and n_pages % PP == 0
    tiles_per_seq = L // tm
    n_steps = B * tiles_per_seq
    sub_per_b = n_pages // PP
    n_sub = Bd * sub_per_b
    assert n_sub % n_steps == 0
    sub_per_step = n_sub // n_steps
    assert sub_per_b % sub_per_step == 0
    b_per = sub_per_b // sub_per_step

    tok_map = lambda g, pt: (g // tiles_per_seq, g % tiles_per_seq, 0)
    feat_map = lambda g, pt: (g // tiles_per_seq, 0, g % tiles_per_seq)
    out_shape, out_specs = _proj_out_specs(B, L, tm, tok_map, feat_map)
    per_b = lambda s: pl.BlockSpec((1,) + s, lambda g, pt: (g // b_per, 0, 0))
    hbm = pl.BlockSpec(memory_space=pl.ANY)
    grid_spec = pltpu.PrefetchScalarGridSpec(
        num_scalar_prefetch=1,
        grid=(n_steps,),
        in_specs=_proj_in_specs(tm, tok_map) + [
            hbm, hbm, hbm, per_b((R, FOX_WIDTH)), per_b((FOX_WIDTH, T)), per_b((T, FOX_WIDTH)), per_b((H, T)),
            per_b((T, FOX_WIDTH))],
        out_specs=out_specs + [per_b((T, FOX_WIDTH))],
        scratch_shapes=[pltpu.VMEM((FOX_HEADS, LANES), F32), pltpu.VMEM((tm, D), BF16),
                        pltpu.VMEM((2, PP, H, HD, P), F32), pltpu.VMEM((2, PP, H, HD, P), F32),
                        pltpu.VMEM((2, PP, H, P), F32), pltpu.SemaphoreType.DMA((3, 2)),
                        pltpu.VMEM((R, LANES), F32), pltpu.VMEM((R, LANES), F32),
                        pltpu.VMEM((R, FOX_WIDTH), F32), pltpu.VMEM((H, LANES), F32)])
    res = pl.pallas_call(
        functools.partial(_proj_dec_kernel, tm=tm, tiles_per_seq=tiles_per_seq, sub_per_step=sub_per_step,
                          sub_per_b=sub_per_b, T=T),
        grid_spec=grid_spec,
        out_shape=out_shape + [jax.ShapeDtypeStruct((Bd, T, FOX_WIDTH), F32)],
        compiler_params=pltpu.CompilerParams(
            dimension_semantics=("arbitrary",), vmem_limit_bytes=VMEM_LIMIT_FUSED),
        name="proj_dec",
    )(page_table, *_proj_args(x, wts), ck, cv, clf, qbd, knT, vn, lfn, fgs3)
    return dict(zip(_PROJ_OUTS, res[:-1])), res[-1]


def _fox_kernel(qT_ref, ktok_ref, cp_ref, vTb_ref, fgs_ref, o_ref, wq_ref, m_ref, acc_ref,
                s0_ref, alpha3_ref, pb3_ref, *, tq):
    qi = pl.program_id(1)
    rows = lax.broadcasted_iota(jnp.int32, (LANES, tq), 0)
    z64 = jnp.zeros((FOX_HEAD_DIM, tq), BF16)

    def bias_rows(h):
        hit = (rows == h) | (rows == h + 8) | (rows == h + 16)
        return jnp.where(hit, -1.0, 0.0).astype(BF16)

    for p in range(N_PAIRS):
        lo = p * LANES
        qe = qT_ref[0, lo:lo + FOX_HEAD_DIM, :]
        qo = qT_ref[0, lo + FOX_HEAD_DIM:lo + LANES, :]
        wq_ref[p] = jnp.concatenate(
            [jnp.concatenate([qe, z64, bias_rows(2 * p)], axis=0),
             jnp.concatenate([z64, qo, bias_rows(2 * p + 1)], axis=0)], axis=1)
    m_ref[...] = jnp.full_like(m_ref, NEG_BIG)
    acc_ref[...] = jnp.zeros_like(acc_ref)

    kpos = lax.broadcasted_iota(jnp.int32, (tq, 2 * tq), 0)
    qpos = lax.broadcasted_iota(jnp.int32, (tq, 2 * tq), 1) % tq
    causal = kpos <= qpos

    def scores(p, k0):
        lo = p * LANES
        ka = jnp.concatenate([ktok_ref[0, pl.ds(k0, tq), lo:lo + LANES],
                              cp_ref[0, pl.ds(k0, tq), :]], axis=1)
        return _dot(ka, wq_ref[p])

    def colmax(p, s):
        m_prev = m_ref[p]
        m_new = jnp.maximum(m_prev, jnp.max(s, axis=0, keepdims=True))
        m_ref[p] = m_new
        return m_prev, m_new

    def expo(s, m_prev, m_new):
        return jnp.exp2(m_prev - m_new), jnp.exp2(s - m_new).astype(BF16)

    def weighted_v(p, k0, alpha, pb):
        for e in range(2):
            h = 2 * p + e
            va = vTb_ref[0, h * V_AUG:(h + 1) * V_AUG, pl.ds(k0, tq)]
            acc_ref[h] = alpha[:, e * tq:(e + 1) * tq] * acc_ref[h] + _dot(va, pb[:, e * tq:(e + 1) * tq])

    def trip(kj, last):
        k0 = pl.multiple_of(kj * tq, tq)
        k_prev = pl.multiple_of(jnp.maximum(kj - 1, 0) * tq, tq)
        s0 = s0_ref[...]
        if last:
            s0 = jnp.where(causal, s0, NEG_BIG)
        mask = (lambda s: jnp.where(causal, s, NEG_BIG)) if last else (lambda s: s)
        mm0 = colmax(0, s0)
        s1 = mask(scores(1, k0))
        e0 = expo(s0, *mm0)
        weighted_v(3, k_prev, alpha3_ref[...], pb3_ref[...])
        mm1 = colmax(1, s1)
        s2 = mask(scores(2, k0))
        e1 = expo(s1, *mm1)
        weighted_v(0, k0, *e0)
        mm2 = colmax(2, s2)
        s3 = mask(scores(3, k0))
        e2 = expo(s2, *mm2)
        weighted_v(1, k0, *e1)
        mm3 = colmax(3, s3)
        if not last:
            s0_ref[...] = scores(0, pl.multiple_of((kj + 1) * tq, tq))
        e3 = expo(s3, *mm3)
        weighted_v(2, k0, *e2)
        alpha3_ref[...], pb3_ref[...] = e3

    s0_ref[...] = scores(0, 0)
    alpha3_ref[...] = jnp.ones_like(alpha3_ref)
    pb3_ref[...] = jnp.zeros_like(pb3_ref)

    def body(kj, carry):
        trip(kj, False)
        return carry

    lax.fori_loop(0, qi, body, 0)
    trip(qi, True)
    weighted_v(3, pl.multiple_of(qi * tq, tq), alpha3_ref[...], pb3_ref[...])

    outs = []
    for h in range(FOX_HEADS):
        a = acc_ref[h]
        outs.append(a[0:FOX_HEAD_DIM] * (1.0 / a[FOX_HEAD_DIM:FOX_HEAD_DIM + 1]))
    oT = jnp.concatenate(outs, axis=0)
    o_ref[0] = (oT.T * fgs_ref[0].astype(F32)).astype(BF16)


def _fox(p):
    tq = FOX_TQ
    B, _, L = p["qT"].shape
    assert L % tq == 0
    return pl.pallas_call(
        functools.partial(_fox_kernel, tq=tq),
        grid=(B, L // tq),
        in_specs=[pl.BlockSpec((1, FOX_WIDTH, tq), lambda b, i: (b, 0, i)),
                  pl.BlockSpec((1, L, FOX_WIDTH), lambda b, i: (b, 0, 0)),
                  pl.BlockSpec((1, L, LANES), lambda b, i: (b, 0, 0)),
                  pl.BlockSpec((1, FOX_HEADS * V_AUG, L), lambda b, i: (b, 0, 0)),
                  pl.BlockSpec((1, tq, FOX_WIDTH), lambda b, i: (b, i, 0))],
        out_specs=pl.BlockSpec((1, tq, FOX_WIDTH), lambda b, i: (b, i, 0)),
        out_shape=jax.ShapeDtypeStruct((B, L, FOX_WIDTH), BF16),
        scratch_shapes=[pltpu.VMEM((N_PAIRS, 2 * LANES, 2 * tq), BF16),
                        pltpu.VMEM((N_PAIRS, 1, 2 * tq), F32),
                        pltpu.VMEM((FOX_HEADS, V_AUG, tq), F32),
                        pltpu.VMEM((tq, 2 * tq), F32), pltpu.VMEM((1, 2 * tq), F32),
                        pltpu.VMEM((tq, 2 * tq), BF16)],
        compiler_params=pltpu.CompilerParams(
            dimension_semantics=("parallel", "arbitrary"), vmem_limit_bytes=VMEM_LIMIT),
        name="fox",
    )(p["qT"], p["ktok"], p["cp"], p["vTb"], p["fgs"])


GLA_MID = GLA_C // 2 - 1


def _gla_consts():
    i = jnp.arange(GLA_C)
    incl = (i[:, None] <= i[None, :]).astype(F32)
    upto_mid = (i[:, None] <= GLA_MID).astype(F32)
    uc = incl - upto_mid
    ud = 1.0 - incl
    ones = jnp.ones((GLA_C, LANES), F32)
    wfeat = jnp.concatenate([uc, ud, ones], axis=1)
    wfeat3 = jnp.concatenate([wfeat] * 3, axis=0).astype(BF16)
    ltok = jnp.concatenate([uc.T, incl.T], axis=0)
    ltok3 = jnp.concatenate([ltok] * 3, axis=1).astype(BF16)
    return wfeat3, ltok3


def _gla_kernel(gq_ref, gkT_ref, gv_ref, la_ref, laT_ref, ggs_ref, gng_ref, wfeat_ref, ltok_ref,
                s0_ref, og_ref, s_ref):
    C = GLA_C

    @pl.when(pl.program_id(1) == 0)
    def _():
        s_ref[...] = s0_ref[...]

    g3 = jnp.concatenate(_split3(la_ref[0]), axis=0).astype(BF16)
    btok = _dot(ltok_ref[...], g3)
    bc, b = btok[0:C], btok[C:2 * C]
    gT3 = jnp.concatenate(_split3(laT_ref[0]), axis=1).astype(BF16)
    bfeat = _dot(gT3, wfeat_ref[...])
    bcT, dT, blast = bfeat[:, 0:C], bfeat[:, C:2 * C], bfeat[:, 2 * C:]

    gq = gq_ref[0].astype(F32)
    q_in = (gq * jnp.exp(bc)).astype(BF16)
    q_s = (gq * jnp.exp(b)).astype(BF16)
    gkT = gkT_ref[0].astype(F32)
    k_inT = (gkT * jnp.exp(-bcT)).astype(BF16)
    k_decT = (gkT * jnp.exp(dT)).astype(BF16)
    decay = jnp.exp(blast)

    ti = lax.broadcasted_iota(jnp.int32, (C, C), 0)
    si = lax.broadcasted_iota(jnp.int32, (C, C), 1)
    tril = si <= ti
    gng = gng_ref[...]
    for h in range(GLA_HEADS):
        ks = slice(h * GLA_DK, (h + 1) * GLA_DK)
        vs = slice(h * GLA_DV, (h + 1) * GLA_DV)
        att = jnp.where(tril, _dot(q_in[:, ks], k_inT[ks, :]), 0.0).astype(BF16)
        vh = gv_ref[0, :, vs]
        s_h = s_ref[0, h]
        o = _dot(att, vh) + _dot(q_s[:, ks], s_h.astype(BF16))
        s_ref[0, h] = jnp.concatenate([decay[ks]] * 2, axis=1) * s_h + _dot(k_decT[ks, :], vh)
        ms = jnp.mean(o * o, axis=-1, keepdims=True)
        on = (o * lax.rsqrt(ms + EPS)) * gng
        og_ref[0, :, vs] = (on * ggs_ref[0, :, vs].astype(F32)).astype(BF16)


def _gla(p, s0, gla_norm_g):
    B, L, _ = p["gq"].shape
    C = GLA_C
    assert L % C == 0
    wfeat3, ltok3 = _gla_consts()
    tok = lambda w: pl.BlockSpec((1, C, w), lambda b, i: (b, i, 0))
    feat = lambda r: pl.BlockSpec((1, r, C), lambda b, i: (b, 0, i))
    st = pl.BlockSpec((1, GLA_HEADS, GLA_DK, GLA_DV), lambda b, i: (b, 0, 0, 0))
    return pl.pallas_call(
        _gla_kernel,
        grid=(B, L // C),
        in_specs=[tok(GLA_KW), feat(GLA_KW), tok(GLA_VW), tok(GLA_KW), feat(GLA_KW), tok(GLA_VW),
                  _const_spec((1, GLA_DV)), _const_spec(wfeat3.shape), _const_spec(ltok3.shape), st],
        out_specs=[tok(GLA_VW), st],
        out_shape=[jax.ShapeDtypeStruct((B, L, GLA_VW), BF16),
                   jax.ShapeDtypeStruct((B, GLA_HEADS, GLA_DK, GLA_DV), F32)],
        compiler_params=pltpu.CompilerParams(
            dimension_semantics=("parallel", "arbitrary"), vmem_limit_bytes=VMEM_LIMIT),
        name="gla",
    )(p["gq"], p["gkT"], p["gv"], p["la"], p["laT"], p["ggs"], gla_norm_g[None, :], wfeat3, ltok3, s0)


def _merge_kernel(x_ref, of_ref, og_ref, sa_ref, sb_ref, wa_ref, wb_ref, wo_ref, y_ref):
    a = _dot(of_ref[0], wa_ref[...])
    b = _dot(og_ref[0], wb_ref[...])
    m = sa_ref[0].astype(F32) * a + sb_ref[0].astype(F32) * b
    y_ref[0] = x_ref[0] + _dot(m.astype(BF16), wo_ref[...])


def _merge(x, of, og, sa, sb, wm, tm):
    B, L, D = x.shape
    assert L % tm == 0
    tok = lambda w: pl.BlockSpec((1, tm, w), lambda b, i: (b, i, 0))
    return pl.pallas_call(
        _merge_kernel,
        grid=(B, L // tm),
        in_specs=[tok(D), tok(FOX_WIDTH), tok(GLA_VW), tok(D), tok(D),
                  _const_spec((FOX_WIDTH, D)), _const_spec((GLA_VW, D)), _const_spec((D, D))],
        out_specs=tok(D),
        out_shape=jax.ShapeDtypeStruct((B, L, D), F32),
        compiler_params=pltpu.CompilerParams(
            dimension_semantics=("parallel", "parallel"), vmem_limit_bytes=VMEM_LIMIT),
        name="merge",
    )(x, of, og, sa, sb, *wm)


def _gla_dec_kernel(gq_ref, gkT_ref, gv_ref, la_ref, laT_ref, ggs_ref, gng_ref, s0_ref, og_ref, s_ref, *, T):
    r8 = lax.broadcasted_iota(jnp.int32, (T, T), 0)
    c8 = lax.broadcasted_iota(jnp.int32, (T, T), 1)
    low = jnp.where(c8 <= r8, 1.0, 0.0).astype(BF16)
    g3 = jnp.concatenate(_split3(la_ref[0]), axis=0).astype(BF16)
    b = _dot(jnp.concatenate([low] * 3, axis=1), g3)
    gT3 = jnp.concatenate(_split3(laT_ref[0]), axis=1).astype(BF16)
    ri = lax.broadcasted_iota(jnp.int32, (T, T + LANES), 0)
    ci = lax.broadcasted_iota(jnp.int32, (T, T + LANES), 1)
    wf = jnp.where((ri <= ci) | (ci >= T), 1.0, 0.0).astype(BF16)
    bf = _dot(gT3, jnp.concatenate([wf] * 3, axis=0))
    bT, blast = bf[:, 0:T], bf[:, T:]
    q_in = (gq_ref[0] * jnp.exp(b)).astype(BF16)
    gkT = gkT_ref[0]
    k_inT = (gkT * jnp.exp(-bT)).astype(BF16)
    k_decT = (gkT * jnp.exp(blast[:, 0:T] - bT)).astype(BF16)
    decay = jnp.exp(blast)
    gng = gng_ref[...]
    for h in range(GLA_HEADS):
        ks = slice(h * GLA_DK, (h + 1) * GLA_DK)
        vs = slice(h * GLA_DV, (h + 1) * GLA_DV)
        att = jnp.where(c8 <= r8, _dot(q_in[:, ks], k_inT[ks, :]), 0.0).astype(BF16)
        vh = gv_ref[0, :, vs].astype(BF16)
        s_h = s0_ref[0, h]
        o = _dot(att, vh) + _dot(q_in[:, ks], s_h.astype(BF16))
        s_ref[0, h] = jnp.concatenate([decay[ks]] * 2, axis=1) * s_h + _dot(k_decT[ks, :], vh)
        ms = jnp.mean(o * o, axis=-1, keepdims=True)
        og_ref[0, :, vs] = ((o * lax.rsqrt(ms + EPS)) * gng) * ggs_ref[0, :, vs]


def _gla_dec(gq, gkT, gv, la, laT, ggs, gla_norm_g, s0):
    Bd, T, _ = gq.shape
    tok = lambda w: pl.BlockSpec((1, T, w), lambda b: (b, 0, 0))
    feat = lambda r: pl.BlockSpec((1, r, T), lambda b: (b, 0, 0))
    st = pl.BlockSpec((1, GLA_HEADS, GLA_DK, GLA_DV), lambda b: (b, 0, 0, 0))
    return pl.pallas_call(
        functools.partial(_gla_dec_kernel, T=T),
        grid=(Bd,),
        in_specs=[tok(GLA_KW), feat(GLA_KW), tok(GLA_VW), tok(GLA_KW), feat(GLA_KW), tok(GLA_VW),
                  _const_spec((1, GLA_DV)), st],
        out_specs=[tok(GLA_VW), st],
        out_shape=[jax.ShapeDtypeStruct((Bd, T, GLA_VW), F32),
                   jax.ShapeDtypeStruct((Bd, GLA_HEADS, GLA_DK, GLA_DV), F32)],
        compiler_params=pltpu.CompilerParams(
            dimension_semantics=("parallel",), vmem_limit_bytes=VMEM_LIMIT),
        name="gla_dec",
    )(gq, gkT, gv, la, laT, ggs, gla_norm_g[None, :], s0)


def _layer(x_prompt, x_sample, wts, wm, cache_k, cache_v, cache_logf, state, page_table, gla_norm_g):
    B, L, D = x_prompt.shape
    Bd, T, _ = x_sample.shape
    N = Bd * T
    H, HD = FOX_HEADS, FOX_HEAD_DIM

    ps = _proj(x_sample.reshape(1, N, D), wts, N)
    feat_bt = lambda a, r: a[0].astype(F32).reshape(r, Bd, T)
    tok_bt = lambda a: a[0].astype(F32).reshape(Bd, T, -1)
    qf = feat_bt(ps["qT"], FOX_WIDTH).reshape(H, HD, Bd, T)
    kf = feat_bt(ps["kT"], FOX_WIDTH).reshape(H, HD, Bd, T)
    vf = feat_bt(ps["vT"], FOX_WIDTH).reshape(H, HD, Bd, T)
    lff = feat_bt(ps["lfT"], H)
    q_bhtd = qf.transpose(2, 0, 3, 1)
    qbd = (q_bhtd[:, :, :, None, :] * jnp.eye(H, dtype=F32)[None, :, None, :, None]
           ).reshape(Bd, H * T, FOX_WIDTH).astype(BF16)
    ck = jnp.transpose(cache_k, (0, 2, 3, 1))
    cv = jnp.transpose(cache_v, (0, 2, 3, 1))
    clf = jnp.transpose(cache_logf, (0, 2, 1))

    pp, of_s = _proj_dec(x_prompt, wts, page_table, ck, cv, clf, qbd,
                         kf.reshape(FOX_WIDTH, Bd, T).transpose(1, 0, 2),
                         vf.reshape(FOX_WIDTH, Bd, T).transpose(1, 2, 0),
                         lff.transpose(1, 0, 2), tok_bt(ps["fgs"]))

    of_p = _fox(pp)
    og_p, s_p = _gla(pp, jnp.zeros((B, GLA_HEADS, GLA_DK, GLA_DV), F32), gla_norm_g)
    y_p = _merge(x_prompt, of_p, og_p, pp["sa"], pp["sb"], wm, MERGE_TM)
    k_p = pp["kT"].reshape(B, H, HD, L).transpose(0, 3, 1, 2)
    v_p = pp["vT"].reshape(B, H, HD, L).transpose(0, 3, 1, 2)
    lf_p = pp["lfT"].transpose(0, 2, 1)

    og_s, s_s = _gla_dec(tok_bt(ps["gq"]), feat_bt(ps["gkT"], GLA_KW).transpose(1, 0, 2), tok_bt(ps["gv"]),
                         tok_bt(ps["la"]), feat_bt(ps["laT"], GLA_KW).transpose(1, 0, 2), tok_bt(ps["ggs"]),
                         gla_norm_g, state)
    y_s = _merge(x_sample.reshape(1, N, D), of_s.reshape(1, N, FOX_WIDTH).astype(BF16),
                 og_s.reshape(1, N, GLA_VW).astype(BF16), ps["sa"], ps["sb"], wm, N).reshape(Bd, T, D)
    k_s = kf.transpose(2, 3, 0, 1)
    v_s = vf.transpose(2, 3, 0, 1)
    lf_s = lff.transpose(1, 2, 0)
    return (y_p, y_s, k_p, v_p, lf_p, s_p, k_s, v_s, lf_s, s_s)


def kernel(x_prompt, x_sample, cache_k, cache_v, cache_logf, state_gla, page_table, ln_g, w_in, fox_b_f, q_norm_g, k_norm_g, gla_w_a2, gla_b_a, gla_norm_g, w_up_a, w_up_b, w_out):
    wts = _prep_weights(ln_g[0], w_in[0], fox_b_f[0], q_norm_g[0], k_norm_g[0], gla_w_a2[0], gla_b_a[0])
    wm = (w_up_a[0].astype(BF16), w_up_b[0].astype(BF16), w_out[0].astype(BF16))
    outs = _layer(x_prompt, x_sample, wts, wm, cache_k[0], cache_v[0], cache_logf[0], state_gla[0],
                  page_table, gla_norm_g[0])
    y_p, y_s = outs[0], outs[1]
    return (y_p, y_s) + tuple(o[None] for o in outs[2:])
```

```python
import functools

import jax
import jax.numpy as jnp
from jax import lax
from jax.experimental import pallas as pl
from jax.experimental.pallas import tpu as pltpu

F32 = jnp.float32
BF16 = jnp.bfloat16

D_MODEL = 1024
FOX_HEADS = 8
FOX_HEAD_DIM = 64
FOX_WIDTH = FOX_HEADS * FOX_HEAD_DIM
FOX_SCALE = FOX_HEAD_DIM ** -0.5
LOG2E = 1.4426950408889634
V_AUG = FOX_HEAD_DIM + 16
GLA_HEADS = 4
GLA_DK = 128
GLA_DV = 256
GLA_KW = GLA_HEADS * GLA_DK
GLA_VW = GLA_HEADS * GLA_DV
GLA_RANK = 16
GLA_TAU = 16.0
EPS = 1e-6
PAGE_SIZE = 128

LANES = 128
SUBLANES = 8
VMEM_LIMIT = 48 * 1024 * 1024
VMEM_LIMIT_FUSED = 56 * 1024 * 1024
PROJ_TM = 256
FOX_TQ = 256
MERGE_TM = 512
GLA_C = 128
DEC_PP = 16

_SIZES = (FOX_WIDTH, FOX_WIDTH, FOX_WIDTH, FOX_HEADS, FOX_WIDTH,
          GLA_KW, GLA_KW, GLA_VW, GLA_RANK, GLA_VW, D_MODEL, D_MODEL)
_OFF = [0]
for _s in _SIZES:
    _OFF.append(_OFF[-1] + _s)
(_FQ, _FK, _FV, _FF, _FG, _GQ, _GK, _GV, _GLR, _GG, _MA, _MB) = _OFF[:-1]

_WF_SMALL = 4 * FOX_WIDTH
_WF_ROWS = _WF_SMALL + 32
_WT_COLS = FOX_WIDTH + GLA_KW + GLA_VW + GLA_VW + 2 * D_MODEL

NEG_BIG = -1e30
N_PAIRS = FOX_HEADS // 2


def _dot_nt(a, b):
    return lax.dot_general(a, b, (((1,), (1,)), ((), ())), preferred_element_type=F32)


def _dot(a, b):
    return jnp.dot(a, b, preferred_element_type=F32)


def _log_sigmoid(x):
    return -(jnp.maximum(-x, 0.0) + jnp.log1p(jnp.exp(-jnp.abs(x))))


def _split3(a):
    hi = a.astype(BF16).astype(F32)
    r = a - hi
    mid = r.astype(BF16).astype(F32)
    lo = (r - mid).astype(BF16).astype(F32)
    return hi, mid, lo


def _split2(a):
    hi = a.astype(BF16).astype(F32)
    return hi, (a - hi).astype(BF16).astype(F32)


def _lane_tile(a, n):
    return a if n == 1 else jnp.concatenate([a] * n, axis=1)


def _const_spec(shape):
    nd = len(shape)
    return pl.BlockSpec(shape, lambda *_: (0,) * nd, pipeline_mode=pl.Buffered(1))


_PROJ_OUTS = ("qT", "kT", "ktok", "cp", "vT", "vTb", "lfT", "fgs", "gq", "gkT", "gv", "laT", "ggs", "sa", "sb")
_N_PROJ_IN = 9


def _proj_stages(in_refs, out_refs, carry_ref, h_ref, first_tile, tm):
    x_ref, lng_ref, wf_ref, wt_ref, qg_ref, kg_ref, fb_ref, wa2t_ref, bat_ref = in_refs
    o = dict(zip(_PROJ_OUTS, out_refs))
    nrep = tm // LANES

    def headnorm(t, g_ref):
        outs = []
        for hh in range(FOX_HEADS):
            blk = t[hh * FOX_HEAD_DIM:(hh + 1) * FOX_HEAD_DIM]
            ssq = jnp.sum(blk * blk, axis=0, keepdims=True) * (1.0 / FOX_HEAD_DIM)
            g = _lane_tile(g_ref[hh * FOX_HEAD_DIM:(hh + 1) * FOX_HEAD_DIM], nrep)
            outs.append((blk * lax.rsqrt(ssq + EPS)) * g)
        return jnp.concatenate(outs, axis=0)

    def tok_group(c0, n, fn, ref):
        for j in range(n // 512):
            z = _dot(h_ref[...], wt_ref[:, c0 + j * 512:c0 + (j + 1) * 512])
            ref[0, :, j * 512:(j + 1) * 512] = fn(z).astype(BF16)

    silu = lambda z: z * jax.nn.sigmoid(z)

    def stage_a():
        @pl.when(first_tile)
        def _():
            carry_ref[...] = jnp.zeros_like(carry_ref)

        x = x_ref[0]
        ms = jnp.mean(x * x, axis=-1, keepdims=True)
        h_ref[...] = ((x * lax.rsqrt(ms + EPS)) * lng_ref[...]).astype(BF16)
        h = h_ref[...]
        q = headnorm(_dot_nt(wf_ref[0:FOX_WIDTH], h), qg_ref)
        o["qT"][0] = (q * (FOX_SCALE * LOG2E)).astype(BF16)
        k = headnorm(_dot_nt(wf_ref[FOX_WIDTH:2 * FOX_WIDTH], h), kg_ref)
        o["kT"][0] = k
        o["ktok"][0] = k.T.astype(BF16)
        v = _dot_nt(wf_ref[2 * FOX_WIDTH:3 * FOX_WIDTH], h)
        o["vT"][0] = v
        ones_rows = jnp.where(lax.broadcasted_iota(jnp.int32, (V_AUG - FOX_HEAD_DIM, tm), 0) == 0, 1.0, 0.0)
        vaug = []
        for hh in range(FOX_HEADS):
            vaug += [v[hh * FOX_HEAD_DIM:(hh + 1) * FOX_HEAD_DIM], ones_rows]
        o["vTb"][0] = jnp.concatenate(vaug, axis=0).astype(BF16)

    def stage_b():
        h = h_ref[...]
        o["gkT"][0] = _dot_nt(wf_ref[3 * FOX_WIDTH:4 * FOX_WIDTH], h).astype(BF16)
        small = _dot_nt(wf_ref[_WF_SMALL:_WF_ROWS], h)
        lf = _log_sigmoid(small[0:FOX_HEADS] + _lane_tile(fb_ref[...], nrep))
        o["lfT"][0] = lf
        pieces = jnp.concatenate(_split3(lf), axis=0).astype(BF16)
        ri = lax.broadcasted_iota(jnp.int32, (tm, tm), 0)
        ci = lax.broadcasted_iota(jnp.int32, (tm, tm), 1)
        utri = jnp.where(ri <= ci, 1.0, 0.0).astype(BF16)
        cum3 = _dot(pieces, utri)
        tot3 = _dot(pieces, jnp.ones((tm, LANES), BF16))
        cum = cum3[0:8] + cum3[8:16] + cum3[16:24]
        tot = tot3[0:8] + tot3[8:16] + tot3[16:24]
        c = cum + _lane_tile(carry_ref[...], nrep)
        carry_ref[...] = carry_ref[...] + tot
        cpieces = jnp.concatenate(list(_split3(c * LOG2E)) + [jnp.zeros((LANES - 24, tm), F32)], axis=0)
        o["cp"][0] = cpieces.T.astype(BF16)
        glr = small[FOX_HEADS:FOX_HEADS + GLA_RANK].astype(BF16)
        pre = _dot(wa2t_ref[...], glr) + _lane_tile(bat_ref[...], nrep)
        o["laT"][0] = _log_sigmoid(pre) * (1.0 / GLA_TAU)
        tok_group(0, FOX_WIDTH, silu, o["fgs"])

    def stage_c():
        tok_group(FOX_WIDTH, GLA_KW, lambda z: z * (GLA_DK ** -0.5), o["gq"])
        tok_group(FOX_WIDTH + GLA_KW, GLA_VW, lambda z: z, o["gv"])
        tok_group(FOX_WIDTH + GLA_KW + GLA_VW, GLA_VW, silu, o["ggs"])

    def stage_d():
        tok_group(FOX_WIDTH + GLA_KW + 2 * GLA_VW, D_MODEL, jax.nn.sigmoid, o["sa"])
        tok_group(FOX_WIDTH + GLA_KW + 2 * GLA_VW + D_MODEL, D_MODEL, jax.nn.sigmoid, o["sb"])

    return stage_a, stage_b, stage_c, stage_d


def _proj_kernel(*refs, tm):
    in_refs = refs[:_N_PROJ_IN]
    out_refs = refs[_N_PROJ_IN:_N_PROJ_IN + len(_PROJ_OUTS)]
    carry_ref, h_ref = refs[_N_PROJ_IN + len(_PROJ_OUTS):]
    for stage in _proj_stages(in_refs, out_refs, carry_ref, h_ref, pl.program_id(1) == 0, tm):
        stage()


def _proj_out_specs(B, L, tm, imap_tok, imap_feat):
    tok = lambda w, dt: (jax.ShapeDtypeStruct((B, L, w), dt), pl.BlockSpec((1, tm, w), imap_tok))
    feat = lambda r, dt: (jax.ShapeDtypeStruct((B, r, L), dt), pl.BlockSpec((1, r, tm), imap_feat))
    outs = dict(
        qT=feat(FOX_WIDTH, BF16), kT=feat(FOX_WIDTH, F32), ktok=tok(FOX_WIDTH, BF16), cp=tok(LANES, BF16),
        vT=feat(FOX_WIDTH, F32), vTb=feat(FOX_HEADS * V_AUG, BF16), lfT=feat(FOX_HEADS, F32), fgs=tok(FOX_WIDTH, BF16),
        gq=tok(GLA_KW, BF16), gkT=feat(GLA_KW, BF16), gv=tok(GLA_VW, BF16),
        laT=feat(GLA_KW, F32), ggs=tok(GLA_VW, BF16), sa=tok(D_MODEL, BF16), sb=tok(D_MODEL, BF16))
    assert tuple(outs) == _PROJ_OUTS
    return [outs[n][0] for n in _PROJ_OUTS], [outs[n][1] for n in _PROJ_OUTS]


def _proj_in_specs(tm, imap_tok):
    D = D_MODEL
    return [pl.BlockSpec((1, tm, D), imap_tok),
            _const_spec((1, D)), _const_spec((_WF_ROWS, D)), _const_spec((D, _WT_COLS)),
            _const_spec((FOX_WIDTH, LANES)), _const_spec((FOX_WIDTH, LANES)),
            _const_spec((FOX_HEADS, LANES)), _const_spec((GLA_KW, GLA_RANK)),
            _const_spec((GLA_KW, LANES))]


def _proj_args(x, wts):
    return (x, wts["ln_g"], wts["wf"], wts["wt"], wts["qg"], wts["kg"], wts["fb"], wts["wa2t"], wts["bat"])


def _proj(x, wts, tm):
    B, L, D = x.shape
    assert L % tm == 0 and tm % LANES == 0
    out_shape, out_specs = _proj_out_specs(B, L, tm, lambda b, i: (b, i, 0), lambda b, i: (b, 0, i))
    res = pl.pallas_call(
        functools.partial(_proj_kernel, tm=tm),
        grid=(B, L // tm),
        in_specs=_proj_in_specs(tm, lambda b, i: (b, i, 0)),
        out_specs=out_specs,
        out_shape=out_shape,
        scratch_shapes=[pltpu.VMEM((FOX_HEADS, LANES), F32), pltpu.VMEM((tm, D), BF16)],
        compiler_params=pltpu.CompilerParams(
            dimension_semantics=("parallel", "arbitrary"), vmem_limit_bytes=VMEM_LIMIT),
        name="proj",
    )(*_proj_args(x, wts))
    return dict(zip(_PROJ_OUTS, res))


def _prep_weights(ln_g, w_in, fox_b_f, q_norm_g, k_norm_g, gla_w_a2, gla_b_a):
    w = w_in
    sl = lambda o, n: w[:, o:o + n]
    small = jnp.concatenate([sl(_FF, FOX_HEADS), sl(_GLR, GLA_RANK), jnp.zeros((D_MODEL, 8), F32)], axis=1)
    wf = jnp.concatenate([sl(_FQ, FOX_WIDTH), sl(_FK, FOX_WIDTH), sl(_FV, FOX_WIDTH), sl(_GK, GLA_KW), small],
                         axis=1).T.astype(BF16)
    wt = jnp.concatenate([sl(_FG, FOX_WIDTH), sl(_GQ, GLA_KW), sl(_GV, GLA_VW), sl(_GG, GLA_VW),
                          sl(_MA, D_MODEL), sl(_MB, D_MODEL)], axis=1).astype(BF16)
    rep = lambda vec: jnp.broadcast_to(vec[:, None], (vec.shape[0], LANES)).astype(F32)
    return dict(ln_g=ln_g[None, :], wf=wf, wt=wt,
                qg=rep(jnp.tile(q_norm_g, FOX_HEADS)), kg=rep(jnp.tile(k_norm_g, FOX_HEADS)),
                fb=rep(fox_b_f), wa2t=gla_w_a2.T.astype(BF16), bat=rep(gla_b_a))


def _dec_substep(kbuf, vbuf, lbuf, qbd_ref, knT_ref, vn_ref, lfn_ref, fgs_ref, o_ref,
                 m_ref, l_ref, acc_ref, carry_ref, first, last, T):
    PP = DEC_PP
    H, P, HD = FOX_HEADS, PAGE_SIZE, FOX_HEAD_DIM
    R = H * T

    @pl.when(first)
    def _():
        m_ref[...] = jnp.full_like(m_ref, NEG_BIG)
        l_ref[...] = jnp.zeros_like(l_ref)
        acc_ref[...] = jnp.zeros_like(acc_ref)
        carry_ref[...] = jnp.zeros_like(carry_ref)

    lf = jnp.concatenate([lbuf[i] for i in range(PP)], axis=0) * LOG2E
    pieces = jnp.concatenate(_split3(lf), axis=0).astype(BF16)
    ri = lax.broadcasted_iota(jnp.int32, (P, 2 * P), 0)
    ci = lax.broadcasted_iota(jnp.int32, (P, 2 * P), 1)
    wcum = jnp.where((ri <= ci) | (ci >= P), 1.0, 0.0).astype(BF16)
    cw3 = _dot(pieces, wcum)
    n = PP * H
    cw = cw3[0:n] + cw3[n:2 * n] + cw3[2 * n:3 * n]
    off = carry_ref[...]
    bias = []
    for i in range(PP):
        c_i = cw[i * H:(i + 1) * H, 0:P] + off
        bias.append(jnp.broadcast_to(c_i[:, None, :], (H, T, P)).reshape(R, P))
        off = off + cw[i * H:(i + 1) * H, P:2 * P]
    carry_ref[...] = off

    def online(s, v_dot):
        m_prev = m_ref[:, 0:1]
        m_new = jnp.maximum(m_prev, jnp.max(s, axis=-1, keepdims=True))
        alpha = jnp.exp2(m_prev - m_new)
        p = jnp.exp2(s - m_new)
        l_new = alpha * l_ref[:, 0:1] + jnp.sum(p, axis=-1, keepdims=True)
        acc_ref[...] = alpha * acc_ref[...] + v_dot(p.astype(BF16))
        m_ref[...] = jnp.broadcast_to(m_new, (R, LANES))
        l_ref[...] = jnp.broadcast_to(l_new, (R, LANES))

    qbd = qbd_ref[0]
    k_all = jnp.concatenate([kbuf[i].reshape(H * HD, P) for i in range(PP)], axis=1).astype(BF16)
    s = _dot(qbd, k_all) - jnp.concatenate(bias, axis=1)
    v_all = jnp.concatenate([vbuf[i].reshape(H * HD, P) for i in range(PP)], axis=1).astype(BF16)
    online(s, lambda pb: _dot_nt(pb, v_all))

    @pl.when(last)
    def _():
        lfn = lfn_ref[0] * LOG2E
        pn = jnp.concatenate(_split3(lfn), axis=0).astype(BF16)
        r8 = lax.broadcasted_iota(jnp.int32, (T, T), 0)
        c8 = lax.broadcasted_iota(jnp.int32, (T, T), 1)
        u8 = jnp.where(r8 <= c8, 1.0, 0.0).astype(BF16)
        cn3 = _dot(pn, u8)
        c_new = carry_ref[:, 0:T] + cn3[0:H] + cn3[H:2 * H] + cn3[2 * H:3 * H]
        bias_n = jnp.broadcast_to(c_new[:, None, :], (H, T, T)).reshape(R, T)
        s_n = _dot(qbd, knT_ref[0].astype(BF16)) - bias_n
        t_row = lax.broadcasted_iota(jnp.int32, (R, T), 0) % T
        t_col = lax.broadcasted_iota(jnp.int32, (R, T), 1)
        s_n = jnp.where(t_col <= t_row, s_n, NEG_BIG)
        vn = vn_ref[0].astype(BF16)
        online(s_n, lambda pb: _dot(pb, vn))
        res = acc_ref[...] / l_ref[:, 0:1]
        lane_head = lax.broadcasted_iota(jnp.int32, (T, H * HD), 1) // HD
        out = jnp.zeros((T, H * HD), F32)
        for h in range(H):
            out = out + jnp.where(lane_head == h, res[h * T:(h + 1) * T], 0.0)
        o_ref[0] = out * fgs_ref[0]


def _fox_stages(qT_ref, ktok_ref, cp_ref, vTb_ref, fgs_ref, o_ref, wq_ref, m_ref, acc_ref,
                s0_ref, alpha3_ref, pb3_ref, qi, tq):
    def causal():
        kpos = lax.broadcasted_iota(jnp.int32, (tq, 2 * tq), 0)
        qpos = lax.broadcasted_iota(jnp.int32, (tq, 2 * tq), 1) % tq
        return kpos <= qpos

    def scores(p, k0):
        lo = p * LANES
        ka = jnp.concatenate([ktok_ref[0, pl.ds(k0, tq), lo:lo + LANES],
                              cp_ref[0, pl.ds(k0, tq), :]], axis=1)
        return _dot(ka, wq_ref[p])

    def colmax(p, s):
        m_prev = m_ref[p]
        m_new = jnp.maximum(m_prev, jnp.max(s, axis=0, keepdims=True))
        m_ref[p] = m_new
        return m_prev, m_new

    def expo(s, m_prev, m_new):
        return jnp.exp2(m_prev - m_new), jnp.exp2(s - m_new).astype(BF16)

    def weighted_v(p, k0, alpha, pb):
        for e in range(2):
            h = 2 * p + e
            va = vTb_ref[0, h * V_AUG:(h + 1) * V_AUG, pl.ds(k0, tq)]
            acc_ref[h] = alpha[:, e * tq:(e + 1) * tq] * acc_ref[h] + _dot(va, pb[:, e * tq:(e + 1) * tq])

    def trip(kj, last):
        k0 = pl.multiple_of(kj * tq, tq)
        k_prev = pl.multiple_of(jnp.maximum(kj - 1, 0) * tq, tq)
        if last:
            keep = causal()
            mask = lambda s: jnp.where(keep, s, NEG_BIG)
        else:
            mask = lambda s: s
        s0 = mask(s0_ref[...])
        mm0 = colmax(0, s0)
        s1 = mask(scores(1, k0))
        e0 = expo(s0, *mm0)
        weighted_v(3, k_prev, alpha3_ref[...], pb3_ref[...])
        mm1 = colmax(1, s1)
        s2 = mask(scores(2, k0))
        e1 = expo(s1, *mm1)
        weighted_v(0, k0, *e0)
        mm2 = colmax(2, s2)
        s3 = mask(scores(3, k0))
        e2 = expo(s2, *mm2)
        weighted_v(1, k0, *e1)
        mm3 = colmax(3, s3)
        if not last:
            s0_ref[...] = scores(0, pl.multiple_of((kj + 1) * tq, tq))
        e3 = expo(s3, *mm3)
        weighted_v(2, k0, *e2)
        alpha3_ref[...], pb3_ref[...] = e3

    def stage_a():
        rows = lax.broadcasted_iota(jnp.int32, (LANES, tq), 0)
        z64 = jnp.zeros((FOX_HEAD_DIM, tq), BF16)

        def bias_rows(h):
            hit = (rows == h) | (rows == h + 8) | (rows == h + 16)
            return jnp.where(hit, -1.0, 0.0).astype(BF16)

        for p in range(N_PAIRS):
            lo = p * LANES
            qe = qT_ref[0, lo:lo + FOX_HEAD_DIM, :]
            qo = qT_ref[0, lo + FOX_HEAD_DIM:lo + LANES, :]
            wq_ref[p] = jnp.concatenate(
                [jnp.concatenate([qe, z64, bias_rows(2 * p)], axis=0),
                 jnp.concatenate([z64, qo, bias_rows(2 * p + 1)], axis=0)], axis=1)
        m_ref[...] = jnp.full_like(m_ref, NEG_BIG)
        acc_ref[...] = jnp.zeros_like(acc_ref)
        s0_ref[...] = scores(0, 0)
        alpha3_ref[...] = jnp.ones_like(alpha3_ref)
        pb3_ref[...] = jnp.zeros_like(pb3_ref)

    def stage_b():
        def body(kj, carry):
            trip(kj, False)
            return carry

        lax.fori_loop(0, qi, body, 0)

    def stage_c():
        trip(qi, True)
        weighted_v(3, pl.multiple_of(qi * tq, tq), alpha3_ref[...], pb3_ref[...])

    def stage_d():
        outs = []
        for h in range(FOX_HEADS):
            a = acc_ref[h]
            outs.append(a[0:FOX_HEAD_DIM] * (1.0 / a[FOX_HEAD_DIM:FOX_HEAD_DIM + 1]))
        oT = jnp.concatenate(outs, axis=0)
        o_ref[0] = (oT.T * fgs_ref[0].astype(F32)).astype(BF16)

    return stage_a, stage_b, stage_c, stage_d


_N_FOX_IN = 5
_N_DEC_IN = 8


def _fox_dec_kernel(pt_ref, *refs, tq, blocks_per_seq, sub_per_step, sub_per_b, T):
    n_in = _N_FOX_IN + _N_DEC_IN
    fox_in = refs[:_N_FOX_IN]
    ck_hbm, cv_hbm, clf_hbm, qbd_ref, knT_ref, vn_ref, lfn_ref, fgs_ref = refs[_N_FOX_IN:n_in]
    o_ref, of_ref = refs[n_in:n_in + 2]
    (wq_ref, m_ref, acc_ref, s0_ref, alpha3_ref, pb3_ref,
     kbuf, vbuf, lbuf, sem, dm_ref, dl_ref, dacc_ref, dcarry_ref) = refs[n_in + 2:]
    g = pl.program_id(0)
    n_sub = pl.num_programs(0) * sub_per_step
    PP = DEC_PP

    def page_copies(s, slot):
        bd = s // sub_per_b
        j = s % sub_per_b
        out = []
        for i in range(PP):
            pg = pt_ref[bd, j * PP + i]
            out.append(pltpu.make_async_copy(ck_hbm.at[pg], kbuf.at[slot, i], sem.at[0, slot]))
            out.append(pltpu.make_async_copy(cv_hbm.at[pg], vbuf.at[slot, i], sem.at[1, slot]))
            out.append(pltpu.make_async_copy(clf_hbm.at[pg], lbuf.at[slot, i], sem.at[2, slot]))
        return out

    @pl.when(g == 0)
    def _():
        for c in page_copies(0, 0):
            c.start()

    stages = _fox_stages(*fox_in, o_ref, wq_ref, m_ref, acc_ref, s0_ref, alpha3_ref, pb3_ref,
                         g % blocks_per_seq, tq)
    assert sub_per_step == len(stages) and sub_per_step % 2 == 0
    for sub in range(sub_per_step):
        s = g * sub_per_step + sub
        slot = sub % 2
        for c in page_copies(s, slot):
            c.wait()
        if sub + 1 < sub_per_step:
            for c in page_copies(s + 1, 1 - slot):
                c.start()
        else:
            @pl.when(s + 1 < n_sub)
            def _():
                for c in page_copies(s + 1, 1 - slot):
                    c.start()
        j = s % sub_per_b
        _dec_substep(kbuf.at[slot], vbuf.at[slot], lbuf.at[slot], qbd_ref, knT_ref, vn_ref, lfn_ref, fgs_ref,
                     of_ref, dm_ref, dl_ref, dacc_ref, dcarry_ref, j == 0, j == sub_per_b - 1, T)
        stages[sub]()


def _fox_dec(p, page_table, ck, cv, clf, qbd, knT, vn, lfn, fgs3):
    tq = FOX_TQ
    B, _, L = p["qT"].shape
    Bd, n_pages = page_table.shape
    T = vn.shape[1]
    H, HD, P, PP = FOX_HEADS, FOX_HEAD_DIM, PAGE_SIZE, DEC_PP
    R = H * T
    assert L % tq == 0 and n_pages % PP == 0
    nq = L // tq
    n_steps = B * nq
    sub_per_b = n_pages // PP
    n_sub = Bd * sub_per_b
    assert n_sub % n_steps == 0
    sub_per_step = n_sub // n_steps
    assert sub_per_b % sub_per_step == 0
    b_per = sub_per_b // sub_per_step

    per_seq = lambda s: pl.BlockSpec((1,) + s, lambda g, pt: (g // nq, 0, 0))
    per_b = lambda s: pl.BlockSpec((1,) + s, lambda g, pt: (g // b_per, 0, 0))
    hbm = pl.BlockSpec(memory_space=pl.ANY)
    grid_spec = pltpu.PrefetchScalarGridSpec(
        num_scalar_prefetch=1,
        grid=(n_steps,),
        in_specs=[pl.BlockSpec((1, FOX_WIDTH, tq), lambda g, pt: (g // nq, 0, g % nq)),
                  per_seq((L, FOX_WIDTH)), per_seq((L, LANES)), per_seq((FOX_HEADS * V_AUG, L)),
                  pl.BlockSpec((1, tq, FOX_WIDTH), lambda g, pt: (g // nq, g % nq, 0)),
                  hbm, hbm, hbm, per_b((R, FOX_WIDTH)), per_b((FOX_WIDTH, T)), per_b((T, FOX_WIDTH)),
                  per_b((H, T)), per_b((T, FOX_WIDTH))],
        out_specs=[pl.BlockSpec((1, tq, FOX_WIDTH), lambda g, pt: (g // nq, g % nq, 0)),
                   per_b((T, FOX_WIDTH))],
        scratch_shapes=[pltpu.VMEM((N_PAIRS, 2 * LANES, 2 * tq), BF16),
                        pltpu.VMEM((N_PAIRS, 1, 2 * tq), F32),
                        pltpu.VMEM((FOX_HEADS, V_AUG, tq), F32),
                        pltpu.VMEM((tq, 2 * tq), F32), pltpu.VMEM((1, 2 * tq), F32),
                        pltpu.VMEM((tq, 2 * tq), BF16),
                        pltpu.VMEM((2, PP, H, HD, P), F32), pltpu.VMEM((2, PP, H, HD, P), F32),
                        pltpu.VMEM((2, PP, H, P), F32), pltpu.SemaphoreType.DMA((3, 2)),
                        pltpu.VMEM((R, LANES), F32), pltpu.VMEM((R, LANES), F32),
                        pltpu.VMEM((R, FOX_WIDTH), F32), pltpu.VMEM((H, LANES), F32)])
    return pl.pallas_call(
        functools.partial(_fox_dec_kernel, tq=tq, blocks_per_seq=nq, sub_per_step=sub_per_step,
                          sub_per_b=sub_per_b, T=T),
        grid_spec=grid_spec,
        out_shape=[jax.ShapeDtypeStruct((B, L, FOX_WIDTH), BF16),
                   jax.ShapeDtypeStruct((Bd, T, FOX_WIDTH), F32)],
        compiler_params=pltpu.CompilerParams(
            dimension_semantics=("arbitrary",), vmem_limit_bytes=VMEM_LIMIT_FUSED),
        name="fox_dec",
    )(page_table, p["qT"], p["ktok"], p["cp"], p["vTb"], p["fgs"], ck, cv, clf, qbd, knT, vn, lfn, fgs3)


GLA_MID = GLA_C // 2 - 1


def _gla_consts():
    i = jnp.arange(GLA_C)
    incl = (i[:, None] <= i[None, :]).astype(F32)
    upto_mid = (i[:, None] <= GLA_MID).astype(F32)
    uc = incl - upto_mid
    ud = 1.0 - incl
    ones = jnp.ones((GLA_C, LANES), F32)
    wfeat = jnp.concatenate([uc, ud, ones], axis=1)
    wfeat2 = jnp.concatenate([wfeat] * 2, axis=0).astype(BF16)
    ltok = jnp.concatenate([uc.T, incl.T], axis=0)
    ltok2 = jnp.concatenate([ltok] * 2, axis=1).astype(BF16)
    return wfeat2, ltok2


def _gla_kernel(gq_ref, gkT_ref, gv_ref, laT_ref, ggs_ref, gng_ref, wfeat_ref, ltok_ref,
                s0_ref, og_ref, s_ref):
    C = GLA_C

    @pl.when(pl.program_id(1) == 0)
    def _():
        s_ref[...] = s0_ref[...]

    laT = laT_ref[0]
    g2 = jnp.concatenate(_split2(laT.T), axis=0).astype(BF16)
    btok = _dot(ltok_ref[...], g2)
    bc, b = btok[0:C], btok[C:2 * C]
    gT2 = jnp.concatenate(_split2(laT), axis=1).astype(BF16)
    bfeat = _dot(gT2, wfeat_ref[...])
    bcT, dT, blast = bfeat[:, 0:C], bfeat[:, C:2 * C], bfeat[:, 2 * C:]

    gq = gq_ref[0].astype(F32)
    q_in = (gq * jnp.exp(bc)).astype(BF16)
    q_s = (gq * jnp.exp(b)).astype(BF16)
    gkT = gkT_ref[0].astype(F32)
    k_inT = (gkT * jnp.exp(-bcT)).astype(BF16)
    k_decT = (gkT * jnp.exp(dT)).astype(BF16)
    decay = jnp.exp(blast)

    ti = lax.broadcasted_iota(jnp.int32, (C, C), 0)
    si = lax.broadcasted_iota(jnp.int32, (C, C), 1)
    tril = si <= ti
    gng = gng_ref[...]
    for h in range(GLA_HEADS):
        ks = slice(h * GLA_DK, (h + 1) * GLA_DK)
        vs = slice(h * GLA_DV, (h + 1) * GLA_DV)
        att = jnp.where(tril, _dot(q_in[:, ks], k_inT[ks, :]), 0.0).astype(BF16)
        vh = gv_ref[0, :, vs]
        s_h = s_ref[0, h]
        o = _dot(att, vh) + _dot(q_s[:, ks], s_h.astype(BF16))
        s_ref[0, h] = jnp.concatenate([decay[ks]] * 2, axis=1) * s_h + _dot(k_decT[ks, :], vh)
        ms = jnp.mean(o * o, axis=-1, keepdims=True)
        on = (o * lax.rsqrt(ms + EPS)) * gng
        og_ref[0, :, vs] = (on * ggs_ref[0, :, vs].astype(F32)).astype(BF16)


def _gla(p, s0, gla_norm_g):
    B, L, _ = p["gq"].shape
    C = GLA_C
    assert L % C == 0
    wfeat2, ltok2 = _gla_consts()
    tok = lambda w: pl.BlockSpec((1, C, w), lambda b, i: (b, i, 0))
    feat = lambda r: pl.BlockSpec((1, r, C), lambda b, i: (b, 0, i))
    st = pl.BlockSpec((1, GLA_HEADS, GLA_DK, GLA_DV), lambda b, i: (b, 0, 0, 0))
    return pl.pallas_call(
        _gla_kernel,
        grid=(B, L // C),
        in_specs=[tok(GLA_KW), feat(GLA_KW), tok(GLA_VW), feat(GLA_KW), tok(GLA_VW),
                  _const_spec((1, GLA_DV)), _const_spec(wfeat2.shape), _const_spec(ltok2.shape), st],
        out_specs=[tok(GLA_VW), st],
        out_shape=[jax.ShapeDtypeStruct((B, L, GLA_VW), BF16),
                   jax.ShapeDtypeStruct((B, GLA_HEADS, GLA_DK, GLA_DV), F32)],
        compiler_params=pltpu.CompilerParams(
            dimension_semantics=("parallel", "arbitrary"), vmem_limit_bytes=VMEM_LIMIT),
        name="gla",
    )(p["gq"], p["gkT"], p["gv"], p["laT"], p["ggs"], gla_norm_g[None, :], wfeat2, ltok2, s0)


def _merge_kernel(x_ref, of_ref, og_ref, sa_ref, sb_ref, wa_ref, wb_ref, wo_ref, y_ref):
    a = _dot(of_ref[0], wa_ref[...])
    b = _dot(og_ref[0], wb_ref[...])
    m = sa_ref[0].astype(F32) * a + sb_ref[0].astype(F32) * b
    y_ref[0] = x_ref[0] + _dot(m.astype(BF16), wo_ref[...])


def _merge(x, of, og, sa, sb, wm, tm):
    B, L, D = x.shape
    assert L % tm == 0
    tok = lambda w: pl.BlockSpec((1, tm, w), lambda b, i: (b, i, 0))
    return pl.pallas_call(
        _merge_kernel,
        grid=(B, L // tm),
        in_specs=[tok(D), tok(FOX_WIDTH), tok(GLA_VW), tok(D), tok(D),
                  _const_spec((FOX_WIDTH, D)), _const_spec((GLA_VW, D)), _const_spec((D, D))],
        out_specs=tok(D),
        out_shape=jax.ShapeDtypeStruct((B, L, D), F32),
        compiler_params=pltpu.CompilerParams(
            dimension_semantics=("parallel", "parallel"), vmem_limit_bytes=VMEM_LIMIT),
        name="merge",
    )(x, of, og, sa, sb, *wm)


def _gla_dec_kernel(gq_ref, gkT_ref, gv_ref, la_ref, laT_ref, ggs_ref, gng_ref, s0_ref, og_ref, s_ref, *, T):
    r8 = lax.broadcasted_iota(jnp.int32, (T, T), 0)
    c8 = lax.broadcasted_iota(jnp.int32, (T, T), 1)
    low = jnp.where(c8 <= r8, 1.0, 0.0).astype(BF16)
    g2 = jnp.concatenate(_split2(la_ref[0]), axis=0).astype(BF16)
    b = _dot(jnp.concatenate([low] * 2, axis=1), g2)
    gT2 = jnp.concatenate(_split2(laT_ref[0]), axis=1).astype(BF16)
    ri = lax.broadcasted_iota(jnp.int32, (T, T + LANES), 0)
    ci = lax.broadcasted_iota(jnp.int32, (T, T + LANES), 1)
    wf = jnp.where((ri <= ci) | (ci >= T), 1.0, 0.0).astype(BF16)
    bf = _dot(gT2, jnp.concatenate([wf] * 2, axis=0))
    bT, blast = bf[:, 0:T], bf[:, T:]
    q_in = (gq_ref[0] * jnp.exp(b)).astype(BF16)
    gkT = gkT_ref[0]
    k_inT = (gkT * jnp.exp(-bT)).astype(BF16)
    k_decT = (gkT * jnp.exp(blast[:, 0:T] - bT)).astype(BF16)
    decay = jnp.exp(blast)
    gng = gng_ref[...]
    for h in range(GLA_HEADS):
        ks = slice(h * GLA_DK, (h + 1) * GLA_DK)
        vs = slice(h * GLA_DV, (h + 1) * GLA_DV)
        att = jnp.where(c8 <= r8, _dot(q_in[:, ks], k_inT[ks, :]), 0.0).astype(BF16)
        vh = gv_ref[0, :, vs].astype(BF16)
        s_h = s0_ref[0, h]
        o = _dot(att, vh) + _dot(q_in[:, ks], s_h.astype(BF16))
        s_ref[0, h] = jnp.concatenate([decay[ks]] * 2, axis=1) * s_h + _dot(k_decT[ks, :], vh)
        ms = jnp.mean(o * o, axis=-1, keepdims=True)
        og_ref[0, :, vs] = ((o * lax.rsqrt(ms + EPS)) * gng) * ggs_ref[0, :, vs]


def _gla_dec(gq, gkT, gv, la, laT, ggs, gla_norm_g, s0):
    Bd, T, _ = gq.shape
    tok = lambda w: pl.BlockSpec((1, T, w), lambda b: (b, 0, 0))
    feat = lambda r: pl.BlockSpec((1, r, T), lambda b: (b, 0, 0))
    st = pl.BlockSpec((1, GLA_HEADS, GLA_DK, GLA_DV), lambda b: (b, 0, 0, 0))
    return pl.pallas_call(
        functools.partial(_gla_dec_kernel, T=T),
        grid=(Bd,),
        in_specs=[tok(GLA_KW), feat(GLA_KW), tok(GLA_VW), tok(GLA_KW), feat(GLA_KW), tok(GLA_VW),
                  _const_spec((1, GLA_DV)), st],
        out_specs=[tok(GLA_VW), st],
        out_shape=[jax.ShapeDtypeStruct((Bd, T, GLA_VW), F32),
                   jax.ShapeDtypeStruct((Bd, GLA_HEADS, GLA_DK, GLA_DV), F32)],
        compiler_params=pltpu.CompilerParams(
            dimension_semantics=("parallel",), vmem_limit_bytes=VMEM_LIMIT),
        name="gla_dec",
    )(gq, gkT, gv, la, laT, ggs, gla_norm_g[None, :], s0)


def _layer(x_prompt, x_sample, wts, wm, cache_k, cache_v, cache_logf, state, page_table, gla_norm_g):
    B, L, D = x_prompt.shape
    Bd, T, _ = x_sample.shape
    N = Bd * T
    H, HD = FOX_HEADS, FOX_HEAD_DIM

    ps = _proj(x_sample.reshape(1, N, D), wts, N)
    feat_bt = lambda a, r: a[0].astype(F32).reshape(r, Bd, T)
    tok_bt = lambda a: a[0].astype(F32).reshape(Bd, T, -1)
    qf = feat_bt(ps["qT"], FOX_WIDTH).reshape(H, HD, Bd, T)
    kf = feat_bt(ps["kT"], FOX_WIDTH).reshape(H, HD, Bd, T)
    vf = feat_bt(ps["vT"], FOX_WIDTH).reshape(H, HD, Bd, T)
    lff = feat_bt(ps["lfT"], H)
    q_bhtd = qf.transpose(2, 0, 3, 1)
    qbd = (q_bhtd[:, :, :, None, :] * jnp.eye(H, dtype=F32)[None, :, None, :, None]
           ).reshape(Bd, H * T, FOX_WIDTH).astype(BF16)
    ck = jnp.transpose(cache_k, (0, 2, 3, 1))
    cv = jnp.transpose(cache_v, (0, 2, 3, 1))
    clf = jnp.transpose(cache_logf, (0, 2, 1))

    pp = _proj(x_prompt, wts, PROJ_TM)
    of_p, of_s = _fox_dec(pp, page_table, ck, cv, clf, qbd,
                          kf.reshape(FOX_WIDTH, Bd, T).transpose(1, 0, 2),
                          vf.reshape(FOX_WIDTH, Bd, T).transpose(1, 2, 0),
                          lff.transpose(1, 0, 2), tok_bt(ps["fgs"]))
    og_p, s_p = _gla(pp, jnp.zeros((B, GLA_HEADS, GLA_DK, GLA_DV), F32), gla_norm_g)
    y_p = _merge(x_prompt, of_p, og_p, pp["sa"], pp["sb"], wm, MERGE_TM)
    k_p = pp["kT"].reshape(B, H, HD, L).transpose(0, 3, 1, 2)
    v_p = pp["vT"].reshape(B, H, HD, L).transpose(0, 3, 1, 2)
    lf_p = pp["lfT"].transpose(0, 2, 1)

    la_s = feat_bt(ps["laT"], GLA_KW)
    og_s, s_s = _gla_dec(tok_bt(ps["gq"]), feat_bt(ps["gkT"], GLA_KW).transpose(1, 0, 2), tok_bt(ps["gv"]),
                         la_s.transpose(1, 2, 0), la_s.transpose(1, 0, 2), tok_bt(ps["ggs"]),
                         gla_norm_g, state)
    y_s = _merge(x_sample.reshape(1, N, D), of_s.reshape(1, N, FOX_WIDTH).astype(BF16),
                 og_s.reshape(1, N, GLA_VW).astype(BF16), ps["sa"], ps["sb"], wm, N).reshape(Bd, T, D)
    k_s = kf.transpose(2, 3, 0, 1)
    v_s = vf.transpose(2, 3, 0, 1)
    lf_s = lff.transpose(1, 2, 0)
    return (y_p, y_s, k_p, v_p, lf_p, s_p, k_s, v_s, lf_s, s_s)


def kernel(x_prompt, x_sample, cache_k, cache_v, cache_logf, state_gla, page_table, ln_g, w_in, fox_b_f, q_norm_g, k_norm_g, gla_w_a2, gla_b_a, gla_norm_g, w_up_a, w_up_b, w_out):
    wts = _prep_weights(ln_g[0], w_in[0], fox_b_f[0], q_norm_g[0], k_norm_g[0], gla_w_a2[0], gla_b_a[0])
    wm = (w_up_a[0].astype(BF16), w_up_b[0].astype(BF16), w_out[0].astype(BF16))
    outs = _layer(x_prompt, x_sample, wts, wm, cache_k[0], cache_v[0], cache_logf[0], state_gla[0],
                  page_table, gla_norm_g[0])
    y_p, y_s = outs[0], outs[1]
    return (y_p, y_s) + tuple(o[None] for o in outs[2:])
```

```python
import functools

import jax
import jax.numpy as jnp
from jax import lax
from jax.experimental import pallas as pl
from jax.experimental.pallas import tpu as pltpu

F32 = jnp.float32
BF16 = jnp.bfloat16

D_MODEL = 1024
FOX_HEADS = 8
FOX_HEAD_DIM = 64
FOX_WIDTH = FOX_HEADS * FOX_HEAD_DIM
FOX_SCALE = FOX_HEAD_DIM ** -0.5
LOG2E = 1.4426950408889634
V_AUG = FOX_HEAD_DIM + 16
GLA_HEADS = 4
GLA_DK = 128
GLA_DV = 256
GLA_KW = GLA_HEADS * GLA_DK
GLA_VW = GLA_HEADS * GLA_DV
GLA_RANK = 16
GLA_TAU = 16.0
EPS = 1e-6
PAGE_SIZE = 128

LANES = 128
SUBLANES = 8
VMEM_LIMIT = 48 * 1024 * 1024
VMEM_LIMIT_FUSED = 56 * 1024 * 1024
PROJ_TM = 256
FOX_TQ = 256
MERGE_TM = 512
GLA_C = 128
DEC_PP = 16
DEC_SLOTS = 4

_SIZES = (FOX_WIDTH, FOX_WIDTH, FOX_WIDTH, FOX_HEADS, FOX_WIDTH,
          GLA_KW, GLA_KW, GLA_VW, GLA_RANK, GLA_VW, D_MODEL, D_MODEL)
_OFF = [0]
for _s in _SIZES:
    _OFF.append(_OFF[-1] + _s)
(_FQ, _FK, _FV, _FF, _FG, _GQ, _GK, _GV, _GLR, _GG, _MA, _MB) = _OFF[:-1]

_WF_SMALL = 4 * FOX_WIDTH
_WF_ROWS = _WF_SMALL + 32
_WT_COLS = FOX_WIDTH + GLA_KW + GLA_VW + GLA_VW + 2 * D_MODEL

NEG_BIG = -1e30
N_PAIRS = FOX_HEADS // 2


def _dot_nt(a, b):
    return lax.dot_general(a, b, (((1,), (1,)), ((), ())), preferred_element_type=F32)


def _dot(a, b):
    return jnp.dot(a, b, preferred_element_type=F32)


def _log_sigmoid(x):
    return -(jnp.maximum(-x, 0.0) + jnp.log1p(jnp.exp(-jnp.abs(x))))


def _split3(a):
    hi = a.astype(BF16).astype(F32)
    r = a - hi
    mid = r.astype(BF16).astype(F32)
    lo = (r - mid).astype(BF16).astype(F32)
    return hi, mid, lo


def _split2(a):
    hi = a.astype(BF16).astype(F32)
    return hi, (a - hi).astype(BF16).astype(F32)


def _lane_tile(a, n):
    return a if n == 1 else jnp.concatenate([a] * n, axis=1)


def _const_spec(shape):
    nd = len(shape)
    return pl.BlockSpec(shape, lambda *_: (0,) * nd, pipeline_mode=pl.Buffered(1))


_PROJ_OUTS = ("qT", "kT", "ktok", "cp", "vT", "vTb", "lfT", "fgs", "gq", "gkT", "gv", "laT", "ggs", "sa", "sb")
_N_PROJ_IN = 9


def _proj_stages(in_refs, out_refs, carry_ref, h_ref, first_tile, tm):
    x_ref, lng_ref, wf_ref, wt_ref, qg_ref, kg_ref, fb_ref, wa2t_ref, bat_ref = in_refs
    o = dict(zip(_PROJ_OUTS, out_refs))
    nrep = tm // LANES

    def headnorm(t, g_ref):
        outs = []
        for hh in range(FOX_HEADS):
            blk = t[hh * FOX_HEAD_DIM:(hh + 1) * FOX_HEAD_DIM]
            ssq = jnp.sum(blk * blk, axis=0, keepdims=True) * (1.0 / FOX_HEAD_DIM)
            g = _lane_tile(g_ref[hh * FOX_HEAD_DIM:(hh + 1) * FOX_HEAD_DIM], nrep)
            outs.append((blk * lax.rsqrt(ssq + EPS)) * g)
        return jnp.concatenate(outs, axis=0)

    def tok_group(c0, n, fn, ref):
        for j in range(n // 512):
            z = _dot(h_ref[...], wt_ref[:, c0 + j * 512:c0 + (j + 1) * 512])
            ref[0, :, j * 512:(j + 1) * 512] = fn(z).astype(BF16)

    silu = lambda z: z * jax.nn.sigmoid(z)

    def stage_a():
        @pl.when(first_tile)
        def _():
            carry_ref[...] = jnp.zeros_like(carry_ref)

        x = x_ref[0]
        ms = jnp.mean(x * x, axis=-1, keepdims=True)
        h_ref[...] = ((x * lax.rsqrt(ms + EPS)) * lng_ref[...]).astype(BF16)
        h = h_ref[...]
        q = headnorm(_dot_nt(wf_ref[0:FOX_WIDTH], h), qg_ref)
        o["qT"][0] = (q * (FOX_SCALE * LOG2E)).astype(BF16)
        k = headnorm(_dot_nt(wf_ref[FOX_WIDTH:2 * FOX_WIDTH], h), kg_ref)
        o["kT"][0] = k
        o["ktok"][0] = k.T.astype(BF16)
        v = _dot_nt(wf_ref[2 * FOX_WIDTH:3 * FOX_WIDTH], h)
        o["vT"][0] = v
        ones_rows = jnp.where(lax.broadcasted_iota(jnp.int32, (V_AUG - FOX_HEAD_DIM, tm), 0) == 0, 1.0, 0.0)
        vaug = []
        for hh in range(FOX_HEADS):
            vaug += [v[hh * FOX_HEAD_DIM:(hh + 1) * FOX_HEAD_DIM], ones_rows]
        o["vTb"][0] = jnp.concatenate(vaug, axis=0).astype(BF16)

    def stage_b():
        h = h_ref[...]
        o["gkT"][0] = _dot_nt(wf_ref[3 * FOX_WIDTH:4 * FOX_WIDTH], h).astype(BF16)
        small = _dot_nt(wf_ref[_WF_SMALL:_WF_ROWS], h)
        lf = _log_sigmoid(small[0:FOX_HEADS] + _lane_tile(fb_ref[...], nrep))
        o["lfT"][0] = lf
        pieces = jnp.concatenate(_split3(lf), axis=0).astype(BF16)
        ri = lax.broadcasted_iota(jnp.int32, (tm, tm), 0)
        ci = lax.broadcasted_iota(jnp.int32, (tm, tm), 1)
        utri = jnp.where(ri <= ci, 1.0, 0.0).astype(BF16)
        cum3 = _dot(pieces, utri)
        tot3 = _dot(pieces, jnp.ones((tm, LANES), BF16))
        cum = cum3[0:8] + cum3[8:16] + cum3[16:24]
        tot = tot3[0:8] + tot3[8:16] + tot3[16:24]
        c = cum + _lane_tile(carry_ref[...], nrep)
        carry_ref[...] = carry_ref[...] + tot
        cpieces = jnp.concatenate(list(_split3(c * LOG2E)) + [jnp.zeros((LANES - 24, tm), F32)], axis=0)
        o["cp"][0] = cpieces.T.astype(BF16)
        glr = small[FOX_HEADS:FOX_HEADS + GLA_RANK].astype(BF16)
        pre = _dot(wa2t_ref[...], glr) + _lane_tile(bat_ref[...], nrep)
        o["laT"][0] = _log_sigmoid(pre) * (1.0 / GLA_TAU)
        tok_group(0, FOX_WIDTH, silu, o["fgs"])

    def stage_c():
        tok_group(FOX_WIDTH, GLA_KW, lambda z: z * (GLA_DK ** -0.5), o["gq"])
        tok_group(FOX_WIDTH + GLA_KW, GLA_VW, lambda z: z, o["gv"])
        tok_group(FOX_WIDTH + GLA_KW + GLA_VW, GLA_VW, silu, o["ggs"])

    def stage_d():
        tok_group(FOX_WIDTH + GLA_KW + 2 * GLA_VW, D_MODEL, jax.nn.sigmoid, o["sa"])
        tok_group(FOX_WIDTH + GLA_KW + 2 * GLA_VW + D_MODEL, D_MODEL, jax.nn.sigmoid, o["sb"])

    return stage_a, stage_b, stage_c, stage_d


def _proj_kernel(*refs, tm):
    in_refs = refs[:_N_PROJ_IN]
    out_refs = refs[_N_PROJ_IN:_N_PROJ_IN + len(_PROJ_OUTS)]
    carry_ref, h_ref = refs[_N_PROJ_IN + len(_PROJ_OUTS):]
    for stage in _proj_stages(in_refs, out_refs, carry_ref, h_ref, pl.program_id(1) == 0, tm):
        stage()


def _proj_out_specs(B, L, tm, imap_tok, imap_feat):
    tok = lambda w, dt: (jax.ShapeDtypeStruct((B, L, w), dt), pl.BlockSpec((1, tm, w), imap_tok))
    feat = lambda r, dt: (jax.ShapeDtypeStruct((B, r, L), dt), pl.BlockSpec((1, r, tm), imap_feat))
    outs = dict(
        qT=feat(FOX_WIDTH, BF16), kT=feat(FOX_WIDTH, F32), ktok=tok(FOX_WIDTH, BF16), cp=tok(LANES, BF16),
        vT=feat(FOX_WIDTH, F32), vTb=feat(FOX_HEADS * V_AUG, BF16), lfT=feat(FOX_HEADS, F32), fgs=tok(FOX_WIDTH, BF16),
        gq=tok(GLA_KW, BF16), gkT=feat(GLA_KW, BF16), gv=tok(GLA_VW, BF16),
        laT=feat(GLA_KW, F32), ggs=tok(GLA_VW, BF16), sa=tok(D_MODEL, BF16), sb=tok(D_MODEL, BF16))
    assert tuple(outs) == _PROJ_OUTS
    return [outs[n][0] for n in _PROJ_OUTS], [outs[n][1] for n in _PROJ_OUTS]


def _proj_in_specs(tm, imap_tok):
    D = D_MODEL
    return [pl.BlockSpec((1, tm, D), imap_tok),
            _const_spec((1, D)), _const_spec((_WF_ROWS, D)), _const_spec((D, _WT_COLS)),
            _const_spec((FOX_WIDTH, LANES)), _const_spec((FOX_WIDTH, LANES)),
            _const_spec((FOX_HEADS, LANES)), _const_spec((GLA_KW, GLA_RANK)),
            _const_spec((GLA_KW, LANES))]


def _proj_args(x, wts):
    return (x, wts["ln_g"], wts["wf"], wts["wt"], wts["qg"], wts["kg"], wts["fb"], wts["wa2t"], wts["bat"])


def _proj(x, wts, tm):
    B, L, D = x.shape
    assert L % tm == 0 and tm % LANES == 0
    out_shape, out_specs = _proj_out_specs(B, L, tm, lambda b, i: (b, i, 0), lambda b, i: (b, 0, i))
    res = pl.pallas_call(
        functools.partial(_proj_kernel, tm=tm),
        grid=(B, L // tm),
        in_specs=_proj_in_specs(tm, lambda b, i: (b, i, 0)),
        out_specs=out_specs,
        out_shape=out_shape,
        scratch_shapes=[pltpu.VMEM((FOX_HEADS, LANES), F32), pltpu.VMEM((tm, D), BF16)],
        compiler_params=pltpu.CompilerParams(
            dimension_semantics=("parallel", "arbitrary"), vmem_limit_bytes=VMEM_LIMIT),
        name="proj",
    )(*_proj_args(x, wts))
    return dict(zip(_PROJ_OUTS, res))


def _prep_weights(ln_g, w_in, fox_b_f, q_norm_g, k_norm_g, gla_w_a2, gla_b_a):
    w = w_in
    sl = lambda o, n: w[:, o:o + n]
    small = jnp.concatenate([sl(_FF, FOX_HEADS), sl(_GLR, GLA_RANK), jnp.zeros((D_MODEL, 8), F32)], axis=1)
    wf = jnp.concatenate([sl(_FQ, FOX_WIDTH), sl(_FK, FOX_WIDTH), sl(_FV, FOX_WIDTH), sl(_GK, GLA_KW), small],
                         axis=1).T.astype(BF16)
    wt = jnp.concatenate([sl(_FG, FOX_WIDTH), sl(_GQ, GLA_KW), sl(_GV, GLA_VW), sl(_GG, GLA_VW),
                          sl(_MA, D_MODEL), sl(_MB, D_MODEL)], axis=1).astype(BF16)
    rep = lambda vec: jnp.broadcast_to(vec[:, None], (vec.shape[0], LANES)).astype(F32)
    return dict(ln_g=ln_g[None, :], wf=wf, wt=wt,
                qg=rep(jnp.tile(q_norm_g, FOX_HEADS)), kg=rep(jnp.tile(k_norm_g, FOX_HEADS)),
                fb=rep(fox_b_f), wa2t=gla_w_a2.T.astype(BF16), bat=rep(gla_b_a))


def _dec_substep(kbuf, vbuf, lbuf, qbd_ref, knT_ref, vn_ref, lfn_ref, fgs_ref, o_ref,
                 m_ref, l_ref, acc_ref, carry_ref, first, last, T):
    PP = DEC_PP
    H, P, HD = FOX_HEADS, PAGE_SIZE, FOX_HEAD_DIM
    R = H * T

    @pl.when(first)
    def _():
        m_ref[...] = jnp.full_like(m_ref, NEG_BIG)
        l_ref[...] = jnp.zeros_like(l_ref)
        acc_ref[...] = jnp.zeros_like(acc_ref)
        carry_ref[...] = jnp.zeros_like(carry_ref)

    lf = jnp.concatenate([lbuf[i] for i in range(PP)], axis=0) * LOG2E
    pieces = jnp.concatenate(_split3(lf), axis=0).astype(BF16)
    ri = lax.broadcasted_iota(jnp.int32, (P, 2 * P), 0)
    ci = lax.broadcasted_iota(jnp.int32, (P, 2 * P), 1)
    wcum = jnp.where((ri <= ci) | (ci >= P), 1.0, 0.0).astype(BF16)
    cw3 = _dot(pieces, wcum)
    n = PP * H
    cw = cw3[0:n] + cw3[n:2 * n] + cw3[2 * n:3 * n]
    off = carry_ref[...]
    bias = []
    for i in range(PP):
        c_i = cw[i * H:(i + 1) * H, 0:P] + off
        bias.append(jnp.broadcast_to(c_i[:, None, :], (H, T, P)).reshape(R, P))
        off = off + cw[i * H:(i + 1) * H, P:2 * P]
    carry_ref[...] = off

    def online(parts):
        m_prev = m_ref[:, 0:1]
        m_new = m_prev
        for s, _ in parts:
            m_new = jnp.maximum(m_new, jnp.max(s, axis=-1, keepdims=True))
        alpha = jnp.exp2(m_prev - m_new)
        l_new = alpha * l_ref[:, 0:1]
        acc = alpha * acc_ref[...]
        for s, v_dot in parts:
            p = jnp.exp2(s - m_new)
            l_new = l_new + jnp.sum(p, axis=-1, keepdims=True)
            acc = acc + v_dot(p.astype(BF16))
        acc_ref[...] = acc
        m_ref[...] = jnp.broadcast_to(m_new, (R, LANES))
        l_ref[...] = jnp.broadcast_to(l_new, (R, LANES))

    def pages(buf, lo, hi):
        return jnp.concatenate([buf[i].reshape(H * HD, P) for i in range(lo, hi)], axis=1).astype(BF16)

    qbd = qbd_ref[0]
    half = PP // 2
    parts = []
    for lo, hi in ((0, half), (half, PP)):
        s = _dot(qbd, pages(kbuf, lo, hi)) - jnp.concatenate(bias[lo:hi], axis=1)
        parts.append((s, lambda pb, lo=lo, hi=hi: _dot_nt(pb, pages(vbuf, lo, hi))))
    online(parts)

    @pl.when(last)
    def _():
        lfn = lfn_ref[0] * LOG2E
        pn = jnp.concatenate(_split3(lfn), axis=0).astype(BF16)
        r8 = lax.broadcasted_iota(jnp.int32, (T, T), 0)
        c8 = lax.broadcasted_iota(jnp.int32, (T, T), 1)
        u8 = jnp.where(r8 <= c8, 1.0, 0.0).astype(BF16)
        cn3 = _dot(pn, u8)
        c_new = carry_ref[:, 0:T] + cn3[0:H] + cn3[H:2 * H] + cn3[2 * H:3 * H]
        bias_n = jnp.broadcast_to(c_new[:, None, :], (H, T, T)).reshape(R, T)
        s_n = _dot(qbd, knT_ref[0].astype(BF16)) - bias_n
        t_row = lax.broadcasted_iota(jnp.int32, (R, T), 0) % T
        t_col = lax.broadcasted_iota(jnp.int32, (R, T), 1)
        s_n = jnp.where(t_col <= t_row, s_n, NEG_BIG)
        vn = vn_ref[0].astype(BF16)
        online([(s_n, lambda pb: _dot(pb, vn))])
        res = acc_ref[...] / l_ref[:, 0:1]
        lane_head = lax.broadcasted_iota(jnp.int32, (T, H * HD), 1) // HD
        out = jnp.zeros((T, H * HD), F32)
        for h in range(H):
            out = out + jnp.where(lane_head == h, res[h * T:(h + 1) * T], 0.0)
        o_ref[0] = out * fgs_ref[0]


def _fox_stages(qT_ref, ktok_ref, cp_ref, vTb_ref, fgs_ref, o_ref, wq_ref, m_ref, acc_ref,
                s0_ref, alpha3_ref, pb3_ref, qi, tq):
    def causal():
        kpos = lax.broadcasted_iota(jnp.int32, (tq, 2 * tq), 0)
        qpos = lax.broadcasted_iota(jnp.int32, (tq, 2 * tq), 1) % tq
        return kpos <= qpos

    def scores(p, k0):
        lo = p * LANES
        ka = jnp.concatenate([ktok_ref[0, pl.ds(k0, tq), lo:lo + LANES],
                              cp_ref[0, pl.ds(k0, tq), :]], axis=1)
        return _dot(ka, wq_ref[p])

    def colmax(p, s):
        m_prev = m_ref[p]
        m_new = jnp.maximum(m_prev, jnp.max(s, axis=0, keepdims=True))
        m_ref[p] = m_new
        return m_prev, m_new

    def expo(s, m_prev, m_new):
        return jnp.exp2(m_prev - m_new), jnp.exp2(s - m_new).astype(BF16)

    def weighted_v(p, k0, alpha, pb):
        for e in range(2):
            h = 2 * p + e
            va = vTb_ref[0, h * V_AUG:(h + 1) * V_AUG, pl.ds(k0, tq)]
            acc_ref[h] = alpha[:, e * tq:(e + 1) * tq] * acc_ref[h] + _dot(va, pb[:, e * tq:(e + 1) * tq])

    def trip(kj, last):
        k0 = pl.multiple_of(kj * tq, tq)
        k_prev = pl.multiple_of(jnp.maximum(kj - 1, 0) * tq, tq)
        if last:
            keep = causal()
            mask = lambda s: jnp.where(keep, s, NEG_BIG)
        else:
            mask = lambda s: s
        s0 = mask(s0_ref[...])
        mm0 = colmax(0, s0)
        s1 = mask(scores(1, k0))
        e0 = expo(s0, *mm0)
        weighted_v(3, k_prev, alpha3_ref[...], pb3_ref[...])
        mm1 = colmax(1, s1)
        s2 = mask(scores(2, k0))
        e1 = expo(s1, *mm1)
        weighted_v(0, k0, *e0)
        mm2 = colmax(2, s2)
        s3 = mask(scores(3, k0))
        e2 = expo(s2, *mm2)
        weighted_v(1, k0, *e1)
        mm3 = colmax(3, s3)
        if not last:
            s0_ref[...] = scores(0, pl.multiple_of((kj + 1) * tq, tq))
        e3 = expo(s3, *mm3)
        weighted_v(2, k0, *e2)
        alpha3_ref[...], pb3_ref[...] = e3

    def stage_a():
        rows = lax.broadcasted_iota(jnp.int32, (LANES, tq), 0)
        z64 = jnp.zeros((FOX_HEAD_DIM, tq), BF16)

        def bias_rows(h):
            hit = (rows == h) | (rows == h + 8) | (rows == h + 16)
            return jnp.where(hit, -1.0, 0.0).astype(BF16)

        for p in range(N_PAIRS):
            lo = p * LANES
            qe = qT_ref[0, lo:lo + FOX_HEAD_DIM, :]
            qo = qT_ref[0, lo + FOX_HEAD_DIM:lo + LANES, :]
            wq_ref[p] = jnp.concatenate(
                [jnp.concatenate([qe, z64, bias_rows(2 * p)], axis=0),
                 jnp.concatenate([z64, qo, bias_rows(2 * p + 1)], axis=0)], axis=1)
        m_ref[...] = jnp.full_like(m_ref, NEG_BIG)
        acc_ref[...] = jnp.zeros_like(acc_ref)
        s0_ref[...] = scores(0, 0)
        alpha3_ref[...] = jnp.ones_like(alpha3_ref)
        pb3_ref[...] = jnp.zeros_like(pb3_ref)

    def stage_b():
        def body(kj, carry):
            trip(kj, False)
            return carry

        lax.fori_loop(0, qi, body, 0)

    def stage_c():
        trip(qi, True)
        weighted_v(3, pl.multiple_of(qi * tq, tq), alpha3_ref[...], pb3_ref[...])

    def stage_d():
        outs = []
        for h in range(FOX_HEADS):
            a = acc_ref[h]
            outs.append(a[0:FOX_HEAD_DIM] * (1.0 / a[FOX_HEAD_DIM:FOX_HEAD_DIM + 1]))
        oT = jnp.concatenate(outs, axis=0)
        o_ref[0] = (oT.T * fgs_ref[0].astype(F32)).astype(BF16)

    return stage_a, stage_b, stage_c, stage_d


_N_FOX_IN = 5
_N_DEC_IN = 8


def _fox_dec_kernel(pt_ref, *refs, tq, blocks_per_seq, sub_per_step, sub_per_b, T):
    n_in = _N_FOX_IN + _N_DEC_IN
    fox_in = refs[:_N_FOX_IN]
    ck_hbm, cv_hbm, clf_hbm, qbd_ref, knT_ref, vn_ref, lfn_ref, fgs_ref = refs[_N_FOX_IN:n_in]
    o_ref, of_ref = refs[n_in:n_in + 2]
    (wq_ref, m_ref, acc_ref, s0_ref, alpha3_ref, pb3_ref,
     kbuf, vbuf, lbuf, sem, dm_ref, dl_ref, dacc_ref, dcarry_ref) = refs[n_in + 2:]
    g = pl.program_id(0)
    n_sub = pl.num_programs(0) * sub_per_step
    PP = DEC_PP

    def page_copies(s, slot):
        bd = s // sub_per_b
        j = s % sub_per_b
        out = []
        for i in range(PP):
            pg = pt_ref[bd, j * PP + i]
            out.append(pltpu.make_async_copy(ck_hbm.at[pg], kbuf.at[slot, i], sem.at[0, slot]))
            out.append(pltpu.make_async_copy(cv_hbm.at[pg], vbuf.at[slot, i], sem.at[1, slot]))
            out.append(pltpu.make_async_copy(clf_hbm.at[pg], lbuf.at[slot, i], sem.at[2, slot]))
        return out

    ahead = DEC_SLOTS - 1

    @pl.when(g == 0)
    def _():
        for a in range(ahead):
            for c in page_copies(a, a):
                c.start()

    stages = _fox_stages(*fox_in, o_ref, wq_ref, m_ref, acc_ref, s0_ref, alpha3_ref, pb3_ref,
                         g % blocks_per_seq, tq)
    assert sub_per_step == len(stages) and sub_per_step >= ahead
    for sub in range(sub_per_step):
        s = g * sub_per_step + sub
        slot = lax.rem(s, DEC_SLOTS)
        for c in page_copies(s, slot):
            c.wait()
        nxt = s + ahead
        if sub + ahead < sub_per_step:
            for c in page_copies(nxt, lax.rem(nxt, DEC_SLOTS)):
                c.start()
        else:
            @pl.when(nxt < n_sub)
            def _():
                for c in page_copies(nxt, lax.rem(nxt, DEC_SLOTS)):
                    c.start()
        j = s % sub_per_b
        _dec_substep(kbuf.at[slot], vbuf.at[slot], lbuf.at[slot], qbd_ref, knT_ref, vn_ref, lfn_ref, fgs_ref,
                     of_ref, dm_ref, dl_ref, dacc_ref, dcarry_ref, j == 0, j == sub_per_b - 1, T)
        stages[sub]()


def _fox_dec(p, page_table, ck, cv, clf, qbd, knT, vn, lfn, fgs3):
    tq = FOX_TQ
    B, _, L = p["qT"].shape
    Bd, n_pages = page_table.shape
    T = vn.shape[1]
    H, HD, P, PP = FOX_HEADS, FOX_HEAD_DIM, PAGE_SIZE, DEC_PP
    R = H * T
    assert L % tq == 0 and n_pages % PP == 0
    nq = L // tq
    n_steps = B * nq
    sub_per_b = n_pages // PP
    n_sub = Bd * sub_per_b
    assert n_sub % n_steps == 0
    sub_per_step = n_sub // n_steps
    assert sub_per_b % sub_per_step == 0
    b_per = sub_per_b // sub_per_step

    per_seq = lambda s: pl.BlockSpec((1,) + s, lambda g, pt: (g // nq, 0, 0), pipeline_mode=pl.Buffered(1))
    per_b = lambda s: pl.BlockSpec((1,) + s, lambda g, pt: (g // b_per, 0, 0))
    hbm = pl.BlockSpec(memory_space=pl.ANY)
    grid_spec = pltpu.PrefetchScalarGridSpec(
        num_scalar_prefetch=1,
        grid=(n_steps,),
        in_specs=[pl.BlockSpec((1, FOX_WIDTH, tq), lambda g, pt: (g // nq, 0, g % nq)),
                  per_seq((L, FOX_WIDTH)), per_seq((L, LANES)), per_seq((FOX_HEADS * V_AUG, L)),
                  pl.BlockSpec((1, tq, FOX_WIDTH), lambda g, pt: (g // nq, g % nq, 0)),
                  hbm, hbm, hbm, per_b((R, FOX_WIDTH)), per_b((FOX_WIDTH, T)), per_b((T, FOX_WIDTH)),
                  per_b((H, T)), per_b((T, FOX_WIDTH))],
        out_specs=[pl.BlockSpec((1, tq, FOX_WIDTH), lambda g, pt: (g // nq, g % nq, 0)),
                   per_b((T, FOX_WIDTH))],
        scratch_shapes=[pltpu.VMEM((N_PAIRS, 2 * LANES, 2 * tq), BF16),
                        pltpu.VMEM((N_PAIRS, 1, 2 * tq), F32),
                        pltpu.VMEM((FOX_HEADS, V_AUG, tq), F32),
                        pltpu.VMEM((tq, 2 * tq), F32), pltpu.VMEM((1, 2 * tq), F32),
                        pltpu.VMEM((tq, 2 * tq), BF16),
                        pltpu.VMEM((DEC_SLOTS, PP, H, HD, P), F32), pltpu.VMEM((DEC_SLOTS, PP, H, HD, P), F32),
                        pltpu.VMEM((DEC_SLOTS, PP, H, P), F32), pltpu.SemaphoreType.DMA((3, DEC_SLOTS)),
                        pltpu.VMEM((R, LANES), F32), pltpu.VMEM((R, LANES), F32),
                        pltpu.VMEM((R, FOX_WIDTH), F32), pltpu.VMEM((H, LANES), F32)])
    return pl.pallas_call(
        functools.partial(_fox_dec_kernel, tq=tq, blocks_per_seq=nq, sub_per_step=sub_per_step,
                          sub_per_b=sub_per_b, T=T),
        grid_spec=grid_spec,
        out_shape=[jax.ShapeDtypeStruct((B, L, FOX_WIDTH), BF16),
                   jax.ShapeDtypeStruct((Bd, T, FOX_WIDTH), F32)],
        compiler_params=pltpu.CompilerParams(
            dimension_semantics=("arbitrary",), vmem_limit_bytes=VMEM_LIMIT_FUSED),
        name="fox_dec",
    )(page_table, p["qT"], p["ktok"], p["cp"], p["vTb"], p["fgs"], ck, cv, clf, qbd, knT, vn, lfn, fgs3)


GLA_MID = GLA_C // 2 - 1


def _gla_consts():
    i = jnp.arange(GLA_C)
    incl = (i[:, None] <= i[None, :]).astype(F32)
    upto_mid = (i[:, None] <= GLA_MID).astype(F32)
    uc = incl - upto_mid
    ud = 1.0 - incl
    ones = jnp.ones((GLA_C, LANES), F32)
    wfeat = jnp.concatenate([uc, ud, ones], axis=1)
    wfeat2 = jnp.concatenate([wfeat] * 2, axis=0).astype(BF16)
    ltok = jnp.concatenate([uc.T, incl.T], axis=0)
    ltok2 = jnp.concatenate([ltok] * 2, axis=1).astype(BF16)
    return wfeat2, ltok2


def _gla_kernel(gq_ref, gkT_ref, gv_ref, laT_ref, ggs_ref, gng_ref, wfeat_ref, ltok_ref,
                s0_ref, og_ref, s_ref):
    C = GLA_C

    @pl.when(pl.program_id(1) == 0)
    def _():
        s_ref[...] = s0_ref[...]

    laT = laT_ref[0]
    g2 = jnp.concatenate(_split2(laT.T), axis=0).astype(BF16)
    btok = _dot(ltok_ref[...], g2)
    bc, b = btok[0:C], btok[C:2 * C]
    gT2 = jnp.concatenate(_split2(laT), axis=1).astype(BF16)
    bfeat = _dot(gT2, wfeat_ref[...])
    bcT, dT, blast = bfeat[:, 0:C], bfeat[:, C:2 * C], bfeat[:, 2 * C:]

    gq = gq_ref[0].astype(F32)
    q_in = (gq * jnp.exp(bc)).astype(BF16)
    q_s = (gq * jnp.exp(b)).astype(BF16)
    gkT = gkT_ref[0].astype(F32)
    k_inT = (gkT * jnp.exp(-bcT)).astype(BF16)
    k_decT = (gkT * jnp.exp(dT)).astype(BF16)
    decay = jnp.exp(blast)

    ti = lax.broadcasted_iota(jnp.int32, (C, C), 0)
    si = lax.broadcasted_iota(jnp.int32, (C, C), 1)
    tril = si <= ti
    gng = gng_ref[...]
    for h in range(GLA_HEADS):
        ks = slice(h * GLA_DK, (h + 1) * GLA_DK)
        vs = slice(h * GLA_DV, (h + 1) * GLA_DV)
        att = jnp.where(tril, _dot(q_in[:, ks], k_inT[ks, :]), 0.0).astype(BF16)
        vh = gv_ref[0, :, vs]
        s_h = s_ref[0, h]
        o = _dot(att, vh) + _dot(q_s[:, ks], s_h.astype(BF16))
        s_ref[0, h] = jnp.concatenate([decay[ks]] * 2, axis=1) * s_h + _dot(k_decT[ks, :], vh)
        ms = jnp.mean(o * o, axis=-1, keepdims=True)
        on = (o * lax.rsqrt(ms + EPS)) * gng
        og_ref[0, :, vs] = (on * ggs_ref[0, :, vs].astype(F32)).astype(BF16)


def _gla(p, s0, gla_norm_g):
    B, L, _ = p["gq"].shape
    C = GLA_C
    assert L % C == 0
    wfeat2, ltok2 = _gla_consts()
    tok = lambda w: pl.BlockSpec((1, C, w), lambda b, i: (b, i, 0))
    feat = lambda r: pl.BlockSpec((1, r, C), lambda b, i: (b, 0, i))
    st = pl.BlockSpec((1, GLA_HEADS, GLA_DK, GLA_DV), lambda b, i: (b, 0, 0, 0))
    return pl.pallas_call(
        _gla_kernel,
        grid=(B, L // C),
        in_specs=[tok(GLA_KW), feat(GLA_KW), tok(GLA_VW), feat(GLA_KW), tok(GLA_VW),
                  _const_spec((1, GLA_DV)), _const_spec(wfeat2.shape), _const_spec(ltok2.shape), st],
        out_specs=[tok(GLA_VW), st],
        out_shape=[jax.ShapeDtypeStruct((B, L, GLA_VW), BF16),
                   jax.ShapeDtypeStruct((B, GLA_HEADS, GLA_DK, GLA_DV), F32)],
        compiler_params=pltpu.CompilerParams(
            dimension_semantics=("parallel", "arbitrary"), vmem_limit_bytes=VMEM_LIMIT),
        name="gla",
    )(p["gq"], p["gkT"], p["gv"], p["laT"], p["ggs"], gla_norm_g[None, :], wfeat2, ltok2, s0)


def _merge_kernel(x_ref, of_ref, og_ref, sa_ref, sb_ref, wa_ref, wb_ref, wo_ref, y_ref):
    a = _dot(of_ref[0], wa_ref[...])
    b = _dot(og_ref[0], wb_ref[...])
    m = sa_ref[0].astype(F32) * a + sb_ref[0].astype(F32) * b
    y_ref[0] = x_ref[0] + _dot(m.astype(BF16), wo_ref[...])


def _merge(x, of, og, sa, sb, wm, tm):
    B, L, D = x.shape
    assert L % tm == 0
    tok = lambda w: pl.BlockSpec((1, tm, w), lambda b, i: (b, i, 0))
    return pl.pallas_call(
        _merge_kernel,
        grid=(B, L // tm),
        in_specs=[tok(D), tok(FOX_WIDTH), tok(GLA_VW), tok(D), tok(D),
                  _const_spec((FOX_WIDTH, D)), _const_spec((GLA_VW, D)), _const_spec((D, D))],
        out_specs=tok(D),
        out_shape=jax.ShapeDtypeStruct((B, L, D), F32),
        compiler_params=pltpu.CompilerParams(
            dimension_semantics=("parallel", "parallel"), vmem_limit_bytes=VMEM_LIMIT),
        name="merge",
    )(x, of, og, sa, sb, *wm)


def _gla_dec_kernel(gq_ref, gkT_ref, gv_ref, la_ref, laT_ref, ggs_ref, gng_ref, s0_ref, og_ref, s_ref, *, T):
    r8 = lax.broadcasted_iota(jnp.int32, (T, T), 0)
    c8 = lax.broadcasted_iota(jnp.int32, (T, T), 1)
    low = jnp.where(c8 <= r8, 1.0, 0.0).astype(BF16)
    g2 = jnp.concatenate(_split2(la_ref[0]), axis=0).astype(BF16)
    b = _dot(jnp.concatenate([low] * 2, axis=1), g2)
    gT2 = jnp.concatenate(_split2(laT_ref[0]), axis=1).astype(BF16)
    ri = lax.broadcasted_iota(jnp.int32, (T, T + LANES), 0)
    ci = lax.broadcasted_iota(jnp.int32, (T, T + LANES), 1)
    wf = jnp.where((ri <= ci) | (ci >= T), 1.0, 0.0).astype(BF16)
    bf = _dot(gT2, jnp.concatenate([wf] * 2, axis=0))
    bT, blast = bf[:, 0:T], bf[:, T:]
    q_in = (gq_ref[0] * jnp.exp(b)).astype(BF16)
    gkT = gkT_ref[0]
    k_inT = (gkT * jnp.exp(-bT)).astype(BF16)
    k_decT = (gkT * jnp.exp(blast[:, 0:T] - bT)).astype(BF16)
    decay = jnp.exp(blast)
    gng = gng_ref[...]
    for h in range(GLA_HEADS):
        ks = slice(h * GLA_DK, (h + 1) * GLA_DK)
        vs = slice(h * GLA_DV, (h + 1) * GLA_DV)
        att = jnp.where(c8 <= r8, _dot(q_in[:, ks], k_inT[ks, :]), 0.0).astype(BF16)
        vh = gv_ref[0, :, vs].astype(BF16)
        s_h = s0_ref[0, h]
        o = _dot(att, vh) + _dot(q_in[:, ks], s_h.astype(BF16))
        s_ref[0, h] = jnp.concatenate([decay[ks]] * 2, axis=1) * s_h + _dot(k_decT[ks, :], vh)
        ms = jnp.mean(o * o, axis=-1, keepdims=True)
        og_ref[0, :, vs] = ((o * lax.rsqrt(ms + EPS)) * gng) * ggs_ref[0, :, vs]


def _gla_dec(gq, gkT, gv, la, laT, ggs, gla_norm_g, s0):
    Bd, T, _ = gq.shape
    tok = lambda w: pl.BlockSpec((1, T, w), lambda b: (b, 0, 0))
    feat = lambda r: pl.BlockSpec((1, r, T), lambda b: (b, 0, 0))
    st = pl.BlockSpec((1, GLA_HEADS, GLA_DK, GLA_DV), lambda b: (b, 0, 0, 0))
    return pl.pallas_call(
        functools.partial(_gla_dec_kernel, T=T),
        grid=(Bd,),
        in_specs=[tok(GLA_KW), feat(GLA_KW), tok(GLA_VW), tok(GLA_KW), feat(GLA_KW), tok(GLA_VW),
                  _const_spec((1, GLA_DV)), st],
        out_specs=[tok(GLA_VW), st],
        out_shape=[jax.ShapeDtypeStruct((Bd, T, GLA_VW), F32),
                   jax.ShapeDtypeStruct((Bd, GLA_HEADS, GLA_DK, GLA_DV), F32)],
        compiler_params=pltpu.CompilerParams(
            dimension_semantics=("parallel",), vmem_limit_bytes=VMEM_LIMIT),
        name="gla_dec",
    )(gq, gkT, gv, la, laT, ggs, gla_norm_g[None, :], s0)


def _layer(x_prompt, x_sample, wts, wm, cache_k, cache_v, cache_logf, state, page_table, gla_norm_g):
    B, L, D = x_prompt.shape
    Bd, T, _ = x_sample.shape
    N = Bd * T
    H, HD = FOX_HEADS, FOX_HEAD_DIM

    ps = _proj(x_sample.reshape(1, N, D), wts, N)
    feat_bt = lambda a, r: a[0].astype(F32).reshape(r, Bd, T)
    tok_bt = lambda a: a[0].astype(F32).reshape(Bd, T, -1)
    qf = feat_bt(ps["qT"], FOX_WIDTH).reshape(H, HD, Bd, T)
    kf = feat_bt(ps["kT"], FOX_WIDTH).reshape(H, HD, Bd, T)
    vf = feat_bt(ps["vT"], FOX_WIDTH).reshape(H, HD, Bd, T)
    lff = feat_bt(ps["lfT"], H)
    q_bhtd = qf.transpose(2, 0, 3, 1)
    qbd = (q_bhtd[:, :, :, None, :] * jnp.eye(H, dtype=F32)[None, :, None, :, None]
           ).reshape(Bd, H * T, FOX_WIDTH).astype(BF16)
    ck = jnp.transpose(cache_k, (0, 2, 3, 1))
    cv = jnp.transpose(cache_v, (0, 2, 3, 1))
    clf = jnp.transpose(cache_logf, (0, 2, 1))

    pp = _proj(x_prompt, wts, PROJ_TM)
    of_p, of_s = _fox_dec(pp, page_table, ck, cv, clf, qbd,
                          kf.reshape(FOX_WIDTH, Bd, T).transpose(1, 0, 2),
                          vf.reshape(FOX_WIDTH, Bd, T).transpose(1, 2, 0),
                          lff.transpose(1, 0, 2), tok_bt(ps["fgs"]))
    og_p, s_p = _gla(pp, jnp.zeros((B, GLA_HEADS, GLA_DK, GLA_DV), F32), gla_norm_g)
    y_p = _merge(x_prompt, of_p, og_p, pp["sa"], pp["sb"], wm, MERGE_TM)
    k_p = pp["kT"].reshape(B, H, HD, L).transpose(0, 3, 1, 2)
    v_p = pp["vT"].reshape(B, H, HD, L).transpose(0, 3, 1, 2)
    lf_p = pp["lfT"].transpose(0, 2, 1)

    la_s = feat_bt(ps["laT"], GLA_KW)
    og_s, s_s = _gla_dec(tok_bt(ps["gq"]), feat_bt(ps["gkT"], GLA_KW).transpose(1, 0, 2), tok_bt(ps["gv"]),
                         la_s.transpose(1, 2, 0), la_s.transpose(1, 0, 2), tok_bt(ps["ggs"]),
                         gla_norm_g, state)
    y_s = _merge(x_sample.reshape(1, N, D), of_s.reshape(1, N, FOX_WIDTH).astype(BF16),
                 og_s.reshape(1, N, GLA_VW).astype(BF16), ps["sa"], ps["sb"], wm, N).reshape(Bd, T, D)
    k_s = kf.transpose(2, 3, 0, 1)
    v_s = vf.transpose(2, 3, 0, 1)
    lf_s = lff.transpose(1, 2, 0)
    return (y_p, y_s, k_p, v_p, lf_p, s_p, k_s, v_s, lf_s, s_s)


def kernel(x_prompt, x_sample, cache_k, cache_v, cache_logf, state_gla, page_table, ln_g, w_in, fox_b_f, q_norm_g, k_norm_g, gla_w_a2, gla_b_a, gla_norm_g, w_up_a, w_up_b, w_out):
    wts = _prep_weights(ln_g[0], w_in[0], fox_b_f[0], q_norm_g[0], k_norm_g[0], gla_w_a2[0], gla_b_a[0])
    wm = (w_up_a[0].astype(BF16), w_up_b[0].astype(BF16), w_out[0].astype(BF16))
    outs = _layer(x_prompt, x_sample, wts, wm, cache_k[0], cache_v[0], cache_logf[0], state_gla[0],
                  page_table, gla_norm_g[0])
    y_p, y_s = outs[0], outs[1]
    return (y_p, y_s) + tuple(o[None] for o in outs[2:])
```

```python
import functools

import jax
import jax.numpy as jnp
from jax import lax
from jax.experimental import pallas as pl
from jax.experimental.pallas import tpu as pltpu

F32 = jnp.float32
BF16 = jnp.bfloat16

D_MODEL = 1024
FOX_HEADS = 8
FOX_HEAD_DIM = 64
FOX_WIDTH = FOX_HEADS * FOX_HEAD_DIM
FOX_SCALE = FOX_HEAD_DIM ** -0.5
LOG2E = 1.4426950408889634
V_AUG = FOX_HEAD_DIM + 16
GLA_HEADS = 4
GLA_DK = 128
GLA_DV = 256
GLA_KW = GLA_HEADS * GLA_DK
GLA_VW = GLA_HEADS * GLA_DV
GLA_RANK = 16
GLA_TAU = 16.0
EPS = 1e-6
PAGE_SIZE = 128

LANES = 128
SUBLANES = 8
VMEM_LIMIT = 48 * 1024 * 1024
VMEM_LIMIT_FUSED = 56 * 1024 * 1024
PROJ_TM = 256
FOX_TQ = 256
MERGE_TM = 512
GLA_C = 128
GLA_CHUNKS_PER_STEP = 4
GLA_DEC_BB = 4
DEC_PP = 16
DEC_SLOTS = 4

_SIZES = (FOX_WIDTH, FOX_WIDTH, FOX_WIDTH, FOX_HEADS, FOX_WIDTH,
          GLA_KW, GLA_KW, GLA_VW, GLA_RANK, GLA_VW, D_MODEL, D_MODEL)
_OFF = [0]
for _s in _SIZES:
    _OFF.append(_OFF[-1] + _s)
(_FQ, _FK, _FV, _FF, _FG, _GQ, _GK, _GV, _GLR, _GG, _MA, _MB) = _OFF[:-1]

_WF_SMALL = 4 * FOX_WIDTH
_WF_ROWS = _WF_SMALL + 32
_WT_COLS = FOX_WIDTH + GLA_KW + GLA_VW + GLA_VW + 2 * D_MODEL

NEG_BIG = -1e30
N_PAIRS = FOX_HEADS // 2


def _dot_nt(a, b):
    return lax.dot_general(a, b, (((1,), (1,)), ((), ())), preferred_element_type=F32)


def _dot(a, b):
    return jnp.dot(a, b, preferred_element_type=F32)


def _log_sigmoid(x):
    return -(jnp.maximum(-x, 0.0) + jnp.log1p(jnp.exp(-jnp.abs(x))))


def _split3(a):
    hi = a.astype(BF16).astype(F32)
    r = a - hi
    mid = r.astype(BF16).astype(F32)
    lo = (r - mid).astype(BF16).astype(F32)
    return hi, mid, lo


def _split2(a):
    hi = a.astype(BF16).astype(F32)
    return hi, (a - hi).astype(BF16).astype(F32)


def _lane_tile(a, n):
    return a if n == 1 else jnp.concatenate([a] * n, axis=1)


def _const_spec(shape):
    nd = len(shape)
    return pl.BlockSpec(shape, lambda *_: (0,) * nd, pipeline_mode=pl.Buffered(1))


_PROJ_OUTS = ("qT", "kT", "ktok", "cp", "vT", "vTb", "lfT", "fgs", "gq", "gkT", "gv", "laT", "ggs", "sa", "sb")
_N_PROJ_IN = 9


def _proj_stages(in_refs, out_refs, carry_ref, h_ref, first_tile, tm):
    x_ref, lng_ref, wf_ref, wt_ref, qg_ref, kg_ref, fb_ref, wa2t_ref, bat_ref = in_refs
    o = dict(zip(_PROJ_OUTS, out_refs))
    nrep = tm // LANES

    def headnorm(t, g_ref):
        outs = []
        for hh in range(FOX_HEADS):
            blk = t[hh * FOX_HEAD_DIM:(hh + 1) * FOX_HEAD_DIM]
            ssq = jnp.sum(blk * blk, axis=0, keepdims=True) * (1.0 / FOX_HEAD_DIM)
            g = _lane_tile(g_ref[hh * FOX_HEAD_DIM:(hh + 1) * FOX_HEAD_DIM], nrep)
            outs.append((blk * lax.rsqrt(ssq + EPS)) * g)
        return jnp.concatenate(outs, axis=0)

    def tok_group(c0, n, fn, ref):
        for j in range(n // 512):
            z = _dot(h_ref[...], wt_ref[:, c0 + j * 512:c0 + (j + 1) * 512])
            ref[0, :, j * 512:(j + 1) * 512] = fn(z).astype(BF16)

    silu = lambda z: z * jax.nn.sigmoid(z)

    def stage_a():
        @pl.when(first_tile)
        def _():
            carry_ref[...] = jnp.zeros_like(carry_ref)

        x = x_ref[0]
        ms = jnp.mean(x * x, axis=-1, keepdims=True)
        h_ref[...] = ((x * lax.rsqrt(ms + EPS)) * lng_ref[...]).astype(BF16)
        h = h_ref[...]
        q = headnorm(_dot_nt(wf_ref[0:FOX_WIDTH], h), qg_ref)
        o["qT"][0] = (q * (FOX_SCALE * LOG2E)).astype(BF16)
        k = headnorm(_dot_nt(wf_ref[FOX_WIDTH:2 * FOX_WIDTH], h), kg_ref)
        o["kT"][0] = k
        o["ktok"][0] = k.T.astype(BF16)
        v = _dot_nt(wf_ref[2 * FOX_WIDTH:3 * FOX_WIDTH], h)
        o["vT"][0] = v
        ones_rows = jnp.where(lax.broadcasted_iota(jnp.int32, (V_AUG - FOX_HEAD_DIM, tm), 0) == 0, 1.0, 0.0)
        vaug = []
        for hh in range(FOX_HEADS):
            vaug += [v[hh * FOX_HEAD_DIM:(hh + 1) * FOX_HEAD_DIM], ones_rows]
        o["vTb"][0] = jnp.concatenate(vaug, axis=0).astype(BF16)

    def stage_b():
        h = h_ref[...]
        o["gkT"][0] = _dot_nt(wf_ref[3 * FOX_WIDTH:4 * FOX_WIDTH], h).astype(BF16)
        small = _dot_nt(wf_ref[_WF_SMALL:_WF_ROWS], h)
        lf = _log_sigmoid(small[0:FOX_HEADS] + _lane_tile(fb_ref[...], nrep))
        o["lfT"][0] = lf
        pieces = jnp.concatenate(_split3(lf), axis=0).astype(BF16)
        ri = lax.broadcasted_iota(jnp.int32, (tm, tm), 0)
        ci = lax.broadcasted_iota(jnp.int32, (tm, tm), 1)
        utri = jnp.where(ri <= ci, 1.0, 0.0).astype(BF16)
        cum3 = _dot(pieces, utri)
        tot3 = _dot(pieces, jnp.ones((tm, LANES), BF16))
        cum = cum3[0:8] + cum3[8:16] + cum3[16:24]
        tot = tot3[0:8] + tot3[8:16] + tot3[16:24]
        c = cum + _lane_tile(carry_ref[...], nrep)
        carry_ref[...] = carry_ref[...] + tot
        cpieces = jnp.concatenate(list(_split3(c * LOG2E)) + [jnp.zeros((LANES - 24, tm), F32)], axis=0)
        o["cp"][0] = cpieces.T.astype(BF16)
        glr = small[FOX_HEADS:FOX_HEADS + GLA_RANK].astype(BF16)
        pre = _dot(wa2t_ref[...], glr) + _lane_tile(bat_ref[...], nrep)
        o["laT"][0] = _log_sigmoid(pre) * (1.0 / GLA_TAU)
        tok_group(0, FOX_WIDTH, silu, o["fgs"])

    def stage_c():
        tok_group(FOX_WIDTH, GLA_KW, lambda z: z * (GLA_DK ** -0.5), o["gq"])
        tok_group(FOX_WIDTH + GLA_KW, GLA_VW, lambda z: z, o["gv"])
        tok_group(FOX_WIDTH + GLA_KW + GLA_VW, GLA_VW, silu, o["ggs"])

    def stage_d():
        tok_group(FOX_WIDTH + GLA_KW + 2 * GLA_VW, D_MODEL, jax.nn.sigmoid, o["sa"])
        tok_group(FOX_WIDTH + GLA_KW + 2 * GLA_VW + D_MODEL, D_MODEL, jax.nn.sigmoid, o["sb"])

    return stage_a, stage_b, stage_c, stage_d


def _proj_kernel(*refs, tm):
    in_refs = refs[:_N_PROJ_IN]
    out_refs = refs[_N_PROJ_IN:_N_PROJ_IN + len(_PROJ_OUTS)]
    carry_ref, h_ref = refs[_N_PROJ_IN + len(_PROJ_OUTS):]
    for stage in _proj_stages(in_refs, out_refs, carry_ref, h_ref, pl.program_id(1) == 0, tm):
        stage()


def _proj_out_specs(B, L, tm, imap_tok, imap_feat):
    tok = lambda w, dt: (jax.ShapeDtypeStruct((B, L, w), dt), pl.BlockSpec((1, tm, w), imap_tok))
    feat = lambda r, dt: (jax.ShapeDtypeStruct((B, r, L), dt), pl.BlockSpec((1, r, tm), imap_feat))
    outs = dict(
        qT=feat(FOX_WIDTH, BF16), kT=feat(FOX_WIDTH, F32), ktok=tok(FOX_WIDTH, BF16), cp=tok(LANES, BF16),
        vT=feat(FOX_WIDTH, F32), vTb=feat(FOX_HEADS * V_AUG, BF16), lfT=feat(FOX_HEADS, F32), fgs=tok(FOX_WIDTH, BF16),
        gq=tok(GLA_KW, BF16), gkT=feat(GLA_KW, BF16), gv=tok(GLA_VW, BF16),
        laT=feat(GLA_KW, F32), ggs=tok(GLA_VW, BF16), sa=tok(D_MODEL, BF16), sb=tok(D_MODEL, BF16))
    assert tuple(outs) == _PROJ_OUTS
    return [outs[n][0] for n in _PROJ_OUTS], [outs[n][1] for n in _PROJ_OUTS]


def _proj_in_specs(tm, imap_tok):
    D = D_MODEL
    return [pl.BlockSpec((1, tm, D), imap_tok),
            _const_spec((1, D)), _const_spec((_WF_ROWS, D)), _const_spec((D, _WT_COLS)),
            _const_spec((FOX_WIDTH, LANES)), _const_spec((FOX_WIDTH, LANES)),
            _const_spec((FOX_HEADS, LANES)), _const_spec((GLA_KW, GLA_RANK)),
            _const_spec((GLA_KW, LANES))]


def _proj_args(x, wts):
    return (x, wts["ln_g"], wts["wf"], wts["wt"], wts["qg"], wts["kg"], wts["fb"], wts["wa2t"], wts["bat"])


def _proj(x, wts, tm):
    B, L, D = x.shape
    assert L % tm == 0 and tm % LANES == 0
    out_shape, out_specs = _proj_out_specs(B, L, tm, lambda b, i: (b, i, 0), lambda b, i: (b, 0, i))
    res = pl.pallas_call(
        functools.partial(_proj_kernel, tm=tm),
        grid=(B, L // tm),
        in_specs=_proj_in_specs(tm, lambda b, i: (b, i, 0)),
        out_specs=out_specs,
        out_shape=out_shape,
        scratch_shapes=[pltpu.VMEM((FOX_HEADS, LANES), F32), pltpu.VMEM((tm, D), BF16)],
        compiler_params=pltpu.CompilerParams(
            dimension_semantics=("parallel", "arbitrary"), vmem_limit_bytes=VMEM_LIMIT),
        name="proj",
    )(*_proj_args(x, wts))
    return dict(zip(_PROJ_OUTS, res))


def _prep_weights(ln_g, w_in, fox_b_f, q_norm_g, k_norm_g, gla_w_a2, gla_b_a):
    w = w_in.T.astype(BF16)
    sl = lambda o, n: w[o:o + n]
    wf = jnp.concatenate([sl(_FQ, FOX_WIDTH), sl(_FK, FOX_WIDTH), sl(_FV, FOX_WIDTH), sl(_GK, GLA_KW),
                          sl(_FF, FOX_HEADS), sl(_GLR, GLA_RANK), jnp.zeros((8, D_MODEL), BF16)], axis=0)
    wt = jnp.concatenate([sl(_FG, FOX_WIDTH), sl(_GQ, GLA_KW), sl(_GV, GLA_VW), sl(_GG, GLA_VW),
                          sl(_MA, D_MODEL), sl(_MB, D_MODEL)], axis=0).T
    rep = lambda vec: jnp.broadcast_to(vec[:, None], (vec.shape[0], LANES)).astype(F32)
    return dict(ln_g=ln_g[None, :], wf=wf, wt=wt,
                qg=rep(jnp.tile(q_norm_g, FOX_HEADS)), kg=rep(jnp.tile(k_norm_g, FOX_HEADS)),
                fb=rep(fox_b_f), wa2t=gla_w_a2.T.astype(BF16), bat=rep(gla_b_a))


def _dec_substep(kbuf, vbuf, lbuf, qbd_ref, knT_ref, vn_ref, lfn_ref, fgs_ref, o_ref,
                 m_ref, l_ref, acc_ref, carry_ref, first, last, T):
    PP = DEC_PP
    H, P, HD = FOX_HEADS, PAGE_SIZE, FOX_HEAD_DIM
    R = H * T

    @pl.when(first)
    def _():
        m_ref[...] = jnp.full_like(m_ref, NEG_BIG)
        l_ref[...] = jnp.zeros_like(l_ref)
        acc_ref[...] = jnp.zeros_like(acc_ref)
        carry_ref[...] = jnp.zeros_like(carry_ref)

    lf = jnp.concatenate([lbuf[i] for i in range(PP)], axis=0) * LOG2E
    pieces = jnp.concatenate(_split3(lf), axis=0).astype(BF16)
    ri = lax.broadcasted_iota(jnp.int32, (P, 2 * P), 0)
    ci = lax.broadcasted_iota(jnp.int32, (P, 2 * P), 1)
    wcum = jnp.where((ri <= ci) | (ci >= P), 1.0, 0.0).astype(BF16)
    cw3 = _dot(pieces, wcum)
    n = PP * H
    cw = cw3[0:n] + cw3[n:2 * n] + cw3[2 * n:3 * n]
    off = carry_ref[...]
    bias = []
    for i in range(PP):
        c_i = cw[i * H:(i + 1) * H, 0:P] + off
        bias.append(jnp.broadcast_to(c_i[:, None, :], (H, T, P)).reshape(R, P))
        off = off + cw[i * H:(i + 1) * H, P:2 * P]
    carry_ref[...] = off

    def online(parts):
        m_prev = m_ref[:, 0:1]
        m_new = m_prev
        for s, _ in parts:
            m_new = jnp.maximum(m_new, jnp.max(s, axis=-1, keepdims=True))
        alpha = jnp.exp2(m_prev - m_new)
        l_new = alpha * l_ref[:, 0:1]
        acc = alpha * acc_ref[...]
        for s, v_dot in parts:
            p = jnp.exp2(s - m_new)
            l_new = l_new + jnp.sum(p, axis=-1, keepdims=True)
            acc = acc + v_dot(p.astype(BF16))
        acc_ref[...] = acc
        m_ref[...] = jnp.broadcast_to(m_new, (R, LANES))
        l_ref[...] = jnp.broadcast_to(l_new, (R, LANES))

    def pages(buf, lo, hi):
        return jnp.concatenate([buf[i].reshape(H * HD, P) for i in range(lo, hi)], axis=1).astype(BF16)

    qbd = qbd_ref[0]
    half = PP // 2
    parts = []
    for lo, hi in ((0, half), (half, PP)):
        s = _dot(qbd, pages(kbuf, lo, hi)) - jnp.concatenate(bias[lo:hi], axis=1)
        parts.append((s, lambda pb, lo=lo, hi=hi: _dot_nt(pb, pages(vbuf, lo, hi))))
    online(parts)

    @pl.when(last)
    def _():
        lfn = lfn_ref[0] * LOG2E
        pn = jnp.concatenate(_split3(lfn), axis=0).astype(BF16)
        r8 = lax.broadcasted_iota(jnp.int32, (T, T), 0)
        c8 = lax.broadcasted_iota(jnp.int32, (T, T), 1)
        u8 = jnp.where(r8 <= c8, 1.0, 0.0).astype(BF16)
        cn3 = _dot(pn, u8)
        c_new = carry_ref[:, 0:T] + cn3[0:H] + cn3[H:2 * H] + cn3[2 * H:3 * H]
        bias_n = jnp.broadcast_to(c_new[:, None, :], (H, T, T)).reshape(R, T)
        s_n = _dot(qbd, knT_ref[0].astype(BF16)) - bias_n
        t_row = lax.broadcasted_iota(jnp.int32, (R, T), 0) % T
        t_col = lax.broadcasted_iota(jnp.int32, (R, T), 1)
        s_n = jnp.where(t_col <= t_row, s_n, NEG_BIG)
        vn = vn_ref[0].astype(BF16)
        online([(s_n, lambda pb: _dot(pb, vn))])
        res = acc_ref[...] / l_ref[:, 0:1]
        lane_head = lax.broadcasted_iota(jnp.int32, (T, H * HD), 1) // HD
        out = jnp.zeros((T, H * HD), F32)
        for h in range(H):
            out = out + jnp.where(lane_head == h, res[h * T:(h + 1) * T], 0.0)
        o_ref[0] = out * fgs_ref[0]


def _fox_stages(qT_ref, ktok_ref, cp_ref, vTb_ref, fgs_ref, o_ref, wq_ref, m_ref, acc_ref,
                s0_ref, alpha3_ref, pb3_ref, qi, tq):
    def causal():
        kpos = lax.broadcasted_iota(jnp.int32, (tq, 2 * tq), 0)
        qpos = lax.broadcasted_iota(jnp.int32, (tq, 2 * tq), 1) % tq
        return kpos <= qpos

    def scores(p, k0):
        lo = p * LANES
        ka = jnp.concatenate([ktok_ref[0, pl.ds(k0, tq), lo:lo + LANES],
                              cp_ref[0, pl.ds(k0, tq), :]], axis=1)
        return _dot(ka, wq_ref[p])

    def colmax(p, s):
        m_prev = m_ref[p]
        m_new = jnp.maximum(m_prev, jnp.max(s, axis=0, keepdims=True))
        m_ref[p] = m_new
        return m_prev, m_new

    def expo(s, m_prev, m_new):
        return jnp.exp2(m_prev - m_new), jnp.exp2(s - m_new).astype(BF16)

    def weighted_v(p, k0, alpha, pb):
        for e in range(2):
            h = 2 * p + e
            va = vTb_ref[0, h * V_AUG:(h + 1) * V_AUG, pl.ds(k0, tq)]
            acc_ref[h] = alpha[:, e * tq:(e + 1) * tq] * acc_ref[h] + _dot(va, pb[:, e * tq:(e + 1) * tq])

    def trip(kj, last):
        k0 = pl.multiple_of(kj * tq, tq)
        k_prev = pl.multiple_of(jnp.maximum(kj - 1, 0) * tq, tq)
        if last:
            keep = causal()
            mask = lambda s: jnp.where(keep, s, NEG_BIG)
        else:
            mask = lambda s: s
        s0 = mask(s0_ref[...])
        mm0 = colmax(0, s0)
        s1 = mask(scores(1, k0))
        e0 = expo(s0, *mm0)
        weighted_v(3, k_prev, alpha3_ref[...], pb3_ref[...])
        mm1 = colmax(1, s1)
        s2 = mask(scores(2, k0))
        e1 = expo(s1, *mm1)
        weighted_v(0, k0, *e0)
        mm2 = colmax(2, s2)
        s3 = mask(scores(3, k0))
        e2 = expo(s2, *mm2)
        weighted_v(1, k0, *e1)
        mm3 = colmax(3, s3)
        if not last:
            s0_ref[...] = scores(0, pl.multiple_of((kj + 1) * tq, tq))
        e3 = expo(s3, *mm3)
        weighted_v(2, k0, *e2)
        alpha3_ref[...], pb3_ref[...] = e3

    def stage_a():
        rows = lax.broadcasted_iota(jnp.int32, (LANES, tq), 0)
        z64 = jnp.zeros((FOX_HEAD_DIM, tq), BF16)

        def bias_rows(h):
            hit = (rows == h) | (rows == h + 8) | (rows == h + 16)
            return jnp.where(hit, -1.0, 0.0).astype(BF16)

        for p in range(N_PAIRS):
            lo = p * LANES
            qe = qT_ref[0, lo:lo + FOX_HEAD_DIM, :]
            qo = qT_ref[0, lo + FOX_HEAD_DIM:lo + LANES, :]
            wq_ref[p] = jnp.concatenate(
                [jnp.concatenate([qe, z64, bias_rows(2 * p)], axis=0),
                 jnp.concatenate([z64, qo, bias_rows(2 * p + 1)], axis=0)], axis=1)
        m_ref[...] = jnp.full_like(m_ref, NEG_BIG)
        acc_ref[...] = jnp.zeros_like(acc_ref)
        s0_ref[...] = scores(0, 0)
        alpha3_ref[...] = jnp.ones_like(alpha3_ref)
        pb3_ref[...] = jnp.zeros_like(pb3_ref)

    def stage_b():
        def body(kj, carry):
            trip(kj, False)
            return carry

        lax.fori_loop(0, qi, body, 0)

    def stage_c():
        trip(qi, True)
        weighted_v(3, pl.multiple_of(qi * tq, tq), alpha3_ref[...], pb3_ref[...])

    def stage_d():
        outs = []
        for h in range(FOX_HEADS):
            a = acc_ref[h]
            outs.append(a[0:FOX_HEAD_DIM] * (1.0 / a[FOX_HEAD_DIM:FOX_HEAD_DIM + 1]))
        oT = jnp.concatenate(outs, axis=0)
        o_ref[0] = (oT.T * fgs_ref[0].astype(F32)).astype(BF16)

    return stage_a, stage_b, stage_c, stage_d


_N_FOX_IN = 5
_N_DEC_IN = 8


def _fox_dec_kernel(pt_ref, *refs, tq, blocks_per_seq, sub_per_step, sub_per_b, T):
    n_in = _N_FOX_IN + _N_DEC_IN
    fox_in = refs[:_N_FOX_IN]
    ck_hbm, cv_hbm, clf_hbm, qbd_ref, knT_ref, vn_ref, lfn_ref, fgs_ref = refs[_N_FOX_IN:n_in]
    o_ref, of_ref = refs[n_in:n_in + 2]
    (wq_ref, m_ref, acc_ref, s0_ref, alpha3_ref, pb3_ref,
     kbuf, vbuf, lbuf, sem, dm_ref, dl_ref, dacc_ref, dcarry_ref) = refs[n_in + 2:]
    g = pl.program_id(0)
    n_sub = pl.num_programs(0) * sub_per_step
    PP = DEC_PP

    def page_copies(s, slot):
        bd = s // sub_per_b
        j = s % sub_per_b
        out = []
        for i in range(PP):
            pg = pt_ref[bd, j * PP + i]
            out.append(pltpu.make_async_copy(ck_hbm.at[pg], kbuf.at[slot, i], sem.at[0, slot]))
            out.append(pltpu.make_async_copy(cv_hbm.at[pg], vbuf.at[slot, i], sem.at[1, slot]))
            out.append(pltpu.make_async_copy(clf_hbm.at[pg], lbuf.at[slot, i], sem.at[2, slot]))
        return out

    ahead = DEC_SLOTS - 1

    @pl.when(g == 0)
    def _():
        for a in range(ahead):
            for c in page_copies(a, a):
                c.start()

    stages = _fox_stages(*fox_in, o_ref, wq_ref, m_ref, acc_ref, s0_ref, alpha3_ref, pb3_ref,
                         g % blocks_per_seq, tq)
    assert sub_per_step == len(stages) and sub_per_step >= ahead
    for sub in range(sub_per_step):
        s = g * sub_per_step + sub
        slot = lax.rem(s, DEC_SLOTS)
        for c in page_copies(s, slot):
            c.wait()
        nxt = s + ahead
        if sub + ahead < sub_per_step:
            for c in page_copies(nxt, lax.rem(nxt, DEC_SLOTS)):
                c.start()
        else:
            @pl.when(nxt < n_sub)
            def _():
                for c in page_copies(nxt, lax.rem(nxt, DEC_SLOTS)):
                    c.start()
        j = s % sub_per_b
        _dec_substep(kbuf.at[slot], vbuf.at[slot], lbuf.at[slot], qbd_ref, knT_ref, vn_ref, lfn_ref, fgs_ref,
                     of_ref, dm_ref, dl_ref, dacc_ref, dcarry_ref, j == 0, j == sub_per_b - 1, T)
        stages[sub]()


def _fox_dec(p, page_table, ck, cv, clf, qbd, knT, vn, lfn, fgs3):
    tq = FOX_TQ
    B, _, L = p["qT"].shape
    Bd, n_pages = page_table.shape
    T = vn.shape[1]
    H, HD, P, PP = FOX_HEADS, FOX_HEAD_DIM, PAGE_SIZE, DEC_PP
    R = H * T
    assert L % tq == 0 and n_pages % PP == 0
    nq = L // tq
    n_steps = B * nq
    sub_per_b = n_pages // PP
    n_sub = Bd * sub_per_b
    assert n_sub % n_steps == 0
    sub_per_step = n_sub // n_steps
    assert sub_per_b % sub_per_step == 0
    b_per = sub_per_b // sub_per_step

    per_seq = lambda s: pl.BlockSpec((1,) + s, lambda g, pt: (g // nq, 0, 0), pipeline_mode=pl.Buffered(1))
    per_b = lambda s: pl.BlockSpec((1,) + s, lambda g, pt: (g // b_per, 0, 0))
    hbm = pl.BlockSpec(memory_space=pl.ANY)
    grid_spec = pltpu.PrefetchScalarGridSpec(
        num_scalar_prefetch=1,
        grid=(n_steps,),
        in_specs=[pl.BlockSpec((1, FOX_WIDTH, tq), lambda g, pt: (g // nq, 0, g % nq)),
                  per_seq((L, FOX_WIDTH)), per_seq((L, LANES)), per_seq((FOX_HEADS * V_AUG, L)),
                  pl.BlockSpec((1, tq, FOX_WIDTH), lambda g, pt: (g // nq, g % nq, 0)),
                  hbm, hbm, hbm, per_b((R, FOX_WIDTH)), per_b((FOX_WIDTH, T)), per_b((T, FOX_WIDTH)),
                  per_b((H, T)), per_b((T, FOX_WIDTH))],
        out_specs=[pl.BlockSpec((1, tq, FOX_WIDTH), lambda g, pt: (g // nq, g % nq, 0)),
                   per_b((T, FOX_WIDTH))],
        scratch_shapes=[pltpu.VMEM((N_PAIRS, 2 * LANES, 2 * tq), BF16),
                        pltpu.VMEM((N_PAIRS, 1, 2 * tq), F32),
                        pltpu.VMEM((FOX_HEADS, V_AUG, tq), F32),
                        pltpu.VMEM((tq, 2 * tq), F32), pltpu.VMEM((1, 2 * tq), F32),
                        pltpu.VMEM((tq, 2 * tq), BF16),
                        pltpu.VMEM((DEC_SLOTS, PP, H, HD, P), F32), pltpu.VMEM((DEC_SLOTS, PP, H, HD, P), F32),
                        pltpu.VMEM((DEC_SLOTS, PP, H, P), F32), pltpu.SemaphoreType.DMA((3, DEC_SLOTS)),
                        pltpu.VMEM((R, LANES), F32), pltpu.VMEM((R, LANES), F32),
                        pltpu.VMEM((R, FOX_WIDTH), F32), pltpu.VMEM((H, LANES), F32)])
    return pl.pallas_call(
        functools.partial(_fox_dec_kernel, tq=tq, blocks_per_seq=nq, sub_per_step=sub_per_step,
                          sub_per_b=sub_per_b, T=T),
        grid_spec=grid_spec,
        out_shape=[jax.ShapeDtypeStruct((B, L, FOX_WIDTH), BF16),
                   jax.ShapeDtypeStruct((Bd, T, FOX_WIDTH), F32)],
        compiler_params=pltpu.CompilerParams(
            dimension_semantics=("arbitrary",), vmem_limit_bytes=VMEM_LIMIT_FUSED),
        name="fox_dec",
    )(page_table, p["qT"], p["ktok"], p["cp"], p["vTb"], p["fgs"], ck, cv, clf, qbd, knT, vn, lfn, fgs3)


GLA_MID = GLA_C // 2 - 1


def _gla_consts():
    i = jnp.arange(GLA_C)
    incl = (i[:, None] <= i[None, :]).astype(F32)
    upto_mid = (i[:, None] <= GLA_MID).astype(F32)
    uc = incl - upto_mid
    ud = 1.0 - incl
    ones = jnp.ones((GLA_C, LANES), F32)
    wfeat = jnp.concatenate([uc, ud, ones], axis=1)
    wfeat2 = jnp.concatenate([wfeat] * 2, axis=0).astype(BF16)
    ltok = jnp.concatenate([uc.T, incl.T], axis=0)
    ltok2 = jnp.concatenate([ltok] * 2, axis=1).astype(BF16)
    return wfeat2, ltok2


def _gla_kernel(gq_ref, gkT_ref, gv_ref, laT_ref, ggs_ref, gng_ref, wfeat_ref, ltok_ref,
                s0_ref, og_ref, s_ref):
    C = GLA_C

    @pl.when(pl.program_id(1) == 0)
    def _():
        s_ref[...] = s0_ref[...]

    ti = lax.broadcasted_iota(jnp.int32, (C, C), 0)
    si = lax.broadcasted_iota(jnp.int32, (C, C), 1)
    tril = si <= ti
    gng = gng_ref[...]
    for c in range(GLA_CHUNKS_PER_STEP):
        tsl = slice(c * C, (c + 1) * C)
        laT = laT_ref[0, :, tsl]
        g2 = jnp.concatenate(_split2(laT.T), axis=0).astype(BF16)
        btok = _dot(ltok_ref[...], g2)
        bc, b = btok[0:C], btok[C:2 * C]
        gT2 = jnp.concatenate(_split2(laT), axis=1).astype(BF16)
        bfeat = _dot(gT2, wfeat_ref[...])
        bcT, dT, blast = bfeat[:, 0:C], bfeat[:, C:2 * C], bfeat[:, 2 * C:]

        gq = gq_ref[0, tsl, :].astype(F32)
        q_in = (gq * jnp.exp(bc)).astype(BF16)
        q_s = (gq * jnp.exp(b)).astype(BF16)
        gkT = gkT_ref[0, :, tsl].astype(F32)
        k_inT = (gkT * jnp.exp(-bcT)).astype(BF16)
        k_decT = (gkT * jnp.exp(dT)).astype(BF16)
        decay = jnp.exp(blast)
        for h in range(GLA_HEADS):
            ks = slice(h * GLA_DK, (h + 1) * GLA_DK)
            vs = slice(h * GLA_DV, (h + 1) * GLA_DV)
            att = jnp.where(tril, _dot(q_in[:, ks], k_inT[ks, :]), 0.0).astype(BF16)
            vh = gv_ref[0, tsl, vs]
            s_h = s_ref[0, h]
            o = _dot(att, vh) + _dot(q_s[:, ks], s_h.astype(BF16))
            s_ref[0, h] = jnp.concatenate([decay[ks]] * 2, axis=1) * s_h + _dot(k_decT[ks, :], vh)
            ms = jnp.mean(o * o, axis=-1, keepdims=True)
            on = (o * lax.rsqrt(ms + EPS)) * gng
            og_ref[0, tsl, vs] = (on * ggs_ref[0, tsl, vs].astype(F32)).astype(BF16)


def _gla(p, s0, gla_norm_g):
    B, L, _ = p["gq"].shape
    C = GLA_C * GLA_CHUNKS_PER_STEP
    assert L % C == 0
    wfeat2, ltok2 = _gla_consts()
    tok = lambda w: pl.BlockSpec((1, C, w), lambda b, i: (b, i, 0))
    feat = lambda r: pl.BlockSpec((1, r, C), lambda b, i: (b, 0, i))
    st = pl.BlockSpec((1, GLA_HEADS, GLA_DK, GLA_DV), lambda b, i: (b, 0, 0, 0))
    return pl.pallas_call(
        _gla_kernel,
        grid=(B, L // C),
        in_specs=[tok(GLA_KW), feat(GLA_KW), tok(GLA_VW), feat(GLA_KW), tok(GLA_VW),
                  _const_spec((1, GLA_DV)), _const_spec(wfeat2.shape), _const_spec(ltok2.shape), st],
        out_specs=[tok(GLA_VW), st],
        out_shape=[jax.ShapeDtypeStruct((B, L, GLA_VW), BF16),
                   jax.ShapeDtypeStruct((B, GLA_HEADS, GLA_DK, GLA_DV), F32)],
        compiler_params=pltpu.CompilerParams(
            dimension_semantics=("parallel", "arbitrary"), vmem_limit_bytes=VMEM_LIMIT),
        name="gla",
    )(p["gq"], p["gkT"], p["gv"], p["laT"], p["ggs"], gla_norm_g[None, :], wfeat2, ltok2, s0)


def _merge_kernel(x_ref, of_ref, og_ref, sa_ref, sb_ref, wa_ref, wb_ref, wo_ref, y_ref):
    a = _dot(of_ref[0], wa_ref[...])
    b = _dot(og_ref[0], wb_ref[...])
    m = sa_ref[0].astype(F32) * a + sb_ref[0].astype(F32) * b
    y_ref[0] = x_ref[0] + _dot(m.astype(BF16), wo_ref[...])


def _merge(x, of, og, sa, sb, wm, tm):
    B, L, D = x.shape
    assert L % tm == 0
    tok = lambda w: pl.BlockSpec((1, tm, w), lambda b, i: (b, i, 0))
    return pl.pallas_call(
        _merge_kernel,
        grid=(B, L // tm),
        in_specs=[tok(D), tok(FOX_WIDTH), tok(GLA_VW), tok(D), tok(D),
                  _const_spec((FOX_WIDTH, D)), _const_spec((GLA_VW, D)), _const_spec((D, D))],
        out_specs=tok(D),
        out_shape=jax.ShapeDtypeStruct((B, L, D), F32),
        compiler_params=pltpu.CompilerParams(
            dimension_semantics=("parallel", "parallel"), vmem_limit_bytes=VMEM_LIMIT),
        name="merge",
    )(x, of, og, sa, sb, *wm)


def _gla_dec_kernel(gq_ref, gkT_ref, gv_ref, la_ref, laT_ref, ggs_ref, gng_ref, s0_ref, og_ref, s_ref, *, T):
    r8 = lax.broadcasted_iota(jnp.int32, (T, T), 0)
    c8 = lax.broadcasted_iota(jnp.int32, (T, T), 1)
    low = jnp.where(c8 <= r8, 1.0, 0.0).astype(BF16)
    ri = lax.broadcasted_iota(jnp.int32, (T, T + LANES), 0)
    ci = lax.broadcasted_iota(jnp.int32, (T, T + LANES), 1)
    wf = jnp.where((ri <= ci) | (ci >= T), 1.0, 0.0).astype(BF16)
    gng = gng_ref[...]
    for bb in range(GLA_DEC_BB):
        g2 = jnp.concatenate(_split2(la_ref[bb]), axis=0).astype(BF16)
        b = _dot(jnp.concatenate([low] * 2, axis=1), g2)
        gT2 = jnp.concatenate(_split2(laT_ref[bb]), axis=1).astype(BF16)
        bf = _dot(gT2, jnp.concatenate([wf] * 2, axis=0))
        bT, blast = bf[:, 0:T], bf[:, T:]
        q_in = (gq_ref[bb] * jnp.exp(b)).astype(BF16)
        gkT = gkT_ref[bb]
        k_inT = (gkT * jnp.exp(-bT)).astype(BF16)
        k_decT = (gkT * jnp.exp(blast[:, 0:T] - bT)).astype(BF16)
        decay = jnp.exp(blast)
        for h in range(GLA_HEADS):
            ks = slice(h * GLA_DK, (h + 1) * GLA_DK)
            vs = slice(h * GLA_DV, (h + 1) * GLA_DV)
            att = jnp.where(c8 <= r8, _dot(q_in[:, ks], k_inT[ks, :]), 0.0).astype(BF16)
            vh = gv_ref[bb, :, vs].astype(BF16)
            s_h = s0_ref[bb, h]
            o = _dot(att, vh) + _dot(q_in[:, ks], s_h.astype(BF16))
            s_ref[bb, h] = jnp.concatenate([decay[ks]] * 2, axis=1) * s_h + _dot(k_decT[ks, :], vh)
            ms = jnp.mean(o * o, axis=-1, keepdims=True)
            og_ref[bb, :, vs] = ((o * lax.rsqrt(ms + EPS)) * gng) * ggs_ref[bb, :, vs]


def _gla_dec(gq, gkT, gv, la, laT, ggs, gla_norm_g, s0):
    Bd, T, _ = gq.shape
    BB = GLA_DEC_BB
    assert Bd % BB == 0
    tok = lambda w: pl.BlockSpec((BB, T, w), lambda b: (b, 0, 0))
    feat = lambda r: pl.BlockSpec((BB, r, T), lambda b: (b, 0, 0))
    st = pl.BlockSpec((BB, GLA_HEADS, GLA_DK, GLA_DV), lambda b: (b, 0, 0, 0))
    return pl.pallas_call(
        functools.partial(_gla_dec_kernel, T=T),
        grid=(Bd // BB,),
        in_specs=[tok(GLA_KW), feat(GLA_KW), tok(GLA_VW), tok(GLA_KW), feat(GLA_KW), tok(GLA_VW),
                  _const_spec((1, GLA_DV)), st],
        out_specs=[tok(GLA_VW), st],
        out_shape=[jax.ShapeDtypeStruct((Bd, T, GLA_VW), F32),
                   jax.ShapeDtypeStruct((Bd, GLA_HEADS, GLA_DK, GLA_DV), F32)],
        compiler_params=pltpu.CompilerParams(
            dimension_semantics=("parallel",), vmem_limit_bytes=VMEM_LIMIT),
        name="gla_dec",
    )(gq, gkT, gv, la, laT, ggs, gla_norm_g[None, :], s0)


def _layer(x_prompt, x_sample, wts, wm, cache_k, cache_v, cache_logf, state, page_table, gla_norm_g):
    B, L, D = x_prompt.shape
    Bd, T, _ = x_sample.shape
    N = Bd * T
    H, HD = FOX_HEADS, FOX_HEAD_DIM

    ps = _proj(x_sample.reshape(1, N, D), wts, N)
    feat_bt = lambda a, r: a[0].astype(F32).reshape(r, Bd, T)
    tok_bt = lambda a: a[0].astype(F32).reshape(Bd, T, -1)
    qf = feat_bt(ps["qT"], FOX_WIDTH).reshape(H, HD, Bd, T)
    kf = feat_bt(ps["kT"], FOX_WIDTH).reshape(H, HD, Bd, T)
    vf = feat_bt(ps["vT"], FOX_WIDTH).reshape(H, HD, Bd, T)
    lff = feat_bt(ps["lfT"], H)
    q_bhtd = qf.transpose(2, 0, 3, 1)
    qbd = (q_bhtd[:, :, :, None, :] * jnp.eye(H, dtype=F32)[None, :, None, :, None]
           ).reshape(Bd, H * T, FOX_WIDTH).astype(BF16)
    ck = jnp.transpose(cache_k, (0, 2, 3, 1))
    cv = jnp.transpose(cache_v, (0, 2, 3, 1))
    clf = jnp.transpose(cache_logf, (0, 2, 1))

    pp = _proj(x_prompt, wts, PROJ_TM)
    of_p, of_s = _fox_dec(pp, page_table, ck, cv, clf, qbd,
                          kf.reshape(FOX_WIDTH, Bd, T).transpose(1, 0, 2),
                          vf.reshape(FOX_WIDTH, Bd, T).transpose(1, 2, 0),
                          lff.transpose(1, 0, 2), tok_bt(ps["fgs"]))
    og_p, s_p = _gla(pp, jnp.zeros((B, GLA_HEADS, GLA_DK, GLA_DV), F32), gla_norm_g)
    y_p = _merge(x_prompt, of_p, og_p, pp["sa"], pp["sb"], wm, MERGE_TM)
    k_p = pp["kT"].reshape(B, H, HD, L).transpose(0, 3, 1, 2)
    v_p = pp["vT"].reshape(B, H, HD, L).transpose(0, 3, 1, 2)
    lf_p = pp["lfT"].transpose(0, 2, 1)

    la_s = feat_bt(ps["laT"], GLA_KW)
    og_s, s_s = _gla_dec(tok_bt(ps["gq"]), feat_bt(ps["gkT"], GLA_KW).transpose(1, 0, 2), tok_bt(ps["gv"]),
                         la_s.transpose(1, 2, 0), la_s.transpose(1, 0, 2), tok_bt(ps["ggs"]),
                         gla_norm_g, state)
    y_s = _merge(x_sample.reshape(1, N, D), of_s.reshape(1, N, FOX_WIDTH).astype(BF16),
                 og_s.reshape(1, N, GLA_VW).astype(BF16), ps["sa"], ps["sb"], wm, N).reshape(Bd, T, D)
    k_s = kf.transpose(2, 3, 0, 1)
    v_s = vf.transpose(2, 3, 0, 1)
    lf_s = lff.transpose(1, 2, 0)
    return (y_p, y_s, k_p, v_p, lf_p, s_p, k_s, v_s, lf_s, s_s)


def kernel(x_prompt, x_sample, cache_k, cache_v, cache_logf, state_gla, page_table, ln_g, w_in, fox_b_f, q_norm_g, k_norm_g, gla_w_a2, gla_b_a, gla_norm_g, w_up_a, w_up_b, w_out):
    wts = _prep_weights(ln_g[0], w_in[0], fox_b_f[0], q_norm_g[0], k_norm_g[0], gla_w_a2[0], gla_b_a[0])
    wm = (w_up_a[0].astype(BF16), w_up_b[0].astype(BF16), w_out[0].astype(BF16))
    outs = _layer(x_prompt, x_sample, wts, wm, cache_k[0], cache_v[0], cache_logf[0], state_gla[0],
                  page_table, gla_norm_g[0])
    y_p, y_s = outs[0], outs[1]
    return (y_p, y_s) + tuple(o[None] for o in outs[2:])
```

```python
import functools

import jax
import jax.numpy as jnp
from jax import lax
from jax.experimental import pallas as pl
from jax.experimental.pallas import tpu as pltpu

F32 = jnp.float32
BF16 = jnp.bfloat16

D_MODEL = 1024
FOX_HEADS = 8
FOX_HEAD_DIM = 64
FOX_WIDTH = FOX_HEADS * FOX_HEAD_DIM
FOX_SCALE = FOX_HEAD_DIM ** -0.5
LOG2E = 1.4426950408889634
V_AUG = FOX_HEAD_DIM + 16
GLA_HEADS = 4
GLA_DK = 128
GLA_DV = 256
GLA_KW = GLA_HEADS * GLA_DK
GLA_VW = GLA_HEADS * GLA_DV
GLA_RANK = 16
GLA_TAU = 16.0
EPS = 1e-6
PAGE_SIZE = 128

LANES = 128
SUBLANES = 8
VMEM_LIMIT = 48 * 1024 * 1024
VMEM_LIMIT_FUSED = 56 * 1024 * 1024
PROJ_TM = 256
FOX_TQ = 256
MERGE_TM = 512
GLA_C = 128
GLA_CHUNKS_PER_STEP = 8
GLA_DEC_BB = 8
DEC_PP = 16
DEC_SLOTS = 4

_SIZES = (FOX_WIDTH, FOX_WIDTH, FOX_WIDTH, FOX_HEADS, FOX_WIDTH,
          GLA_KW, GLA_KW, GLA_VW, GLA_RANK, GLA_VW, D_MODEL, D_MODEL)
_OFF = [0]
for _s in _SIZES:
    _OFF.append(_OFF[-1] + _s)
(_FQ, _FK, _FV, _FF, _FG, _GQ, _GK, _GV, _GLR, _GG, _MA, _MB) = _OFF[:-1]

_WF_SMALL = 4 * FOX_WIDTH
_WF_ROWS = _WF_SMALL + 32
_WT_COLS = FOX_WIDTH + GLA_KW + GLA_VW + GLA_VW + 2 * D_MODEL

NEG_BIG = -1e30
N_PAIRS = FOX_HEADS // 2


def _dot_nt(a, b):
    return lax.dot_general(a, b, (((1,), (1,)), ((), ())), preferred_element_type=F32)


def _dot(a, b):
    return jnp.dot(a, b, preferred_element_type=F32)


def _log_sigmoid(x):
    return -(jnp.maximum(-x, 0.0) + jnp.log1p(jnp.exp(-jnp.abs(x))))


def _split3(a):
    hi = a.astype(BF16).astype(F32)
    r = a - hi
    mid = r.astype(BF16).astype(F32)
    lo = (r - mid).astype(BF16).astype(F32)
    return hi, mid, lo


def _split2(a):
    hi = a.astype(BF16).astype(F32)
    return hi, (a - hi).astype(BF16).astype(F32)


def _lane_tile(a, n):
    return a if n == 1 else jnp.concatenate([a] * n, axis=1)


def _const_spec(shape):
    nd = len(shape)
    return pl.BlockSpec(shape, lambda *_: (0,) * nd, pipeline_mode=pl.Buffered(1))


_PROJ_OUTS = ("qT", "kT", "ktok", "cp", "vT", "vTb", "lfT", "fgs", "gq", "gkT", "gv", "laT", "ggs", "sa", "sb")
_N_PROJ_IN = 9


def _proj_body(in_refs, out_refs, carry_ref, h_ref, first_tile, tm):
    x_ref, lng_ref, wf_ref, wt_ref, qg_ref, kg_ref, fb_ref, wa2t_ref, bat_ref = in_refs
    o = dict(zip(_PROJ_OUTS, out_refs))
    nrep = tm // LANES

    @pl.when(first_tile)
    def _():
        carry_ref[...] = jnp.zeros_like(carry_ref)

    x = x_ref[0]
    ms = jnp.mean(x * x, axis=-1, keepdims=True)
    h_ref[...] = ((x * lax.rsqrt(ms + EPS)) * lng_ref[...]).astype(BF16)

    def headnorm(t, g_ref):
        outs = []
        for hh in range(FOX_HEADS):
            blk = t[hh * FOX_HEAD_DIM:(hh + 1) * FOX_HEAD_DIM]
            ssq = jnp.sum(blk * blk, axis=0, keepdims=True) * (1.0 / FOX_HEAD_DIM)
            g = _lane_tile(g_ref[hh * FOX_HEAD_DIM:(hh + 1) * FOX_HEAD_DIM], nrep)
            outs.append((blk * lax.rsqrt(ssq + EPS)) * g)
        return jnp.concatenate(outs, axis=0)

    def feat_dot(r0, r1):
        return lambda: _dot_nt(wf_ref[r0:r1], h_ref[...])

    def q_out(z):
        o["qT"][0] = (headnorm(z, qg_ref) * (FOX_SCALE * LOG2E)).astype(BF16)

    def k_out(z):
        k = headnorm(z, kg_ref)
        o["kT"][0] = k
        o["ktok"][0] = k.T.astype(BF16)

    def v_out(v):
        o["vT"][0] = v
        ones_rows = jnp.where(lax.broadcasted_iota(jnp.int32, (V_AUG - FOX_HEAD_DIM, tm), 0) == 0, 1.0, 0.0)
        vaug = []
        for hh in range(FOX_HEADS):
            vaug += [v[hh * FOX_HEAD_DIM:(hh + 1) * FOX_HEAD_DIM], ones_rows]
        o["vTb"][0] = jnp.concatenate(vaug, axis=0).astype(BF16)

    def gk_out(z):
        o["gkT"][0] = z.astype(BF16)

    def small_out(small):
        lf = _log_sigmoid(small[0:FOX_HEADS] + _lane_tile(fb_ref[...], nrep))
        o["lfT"][0] = lf
        pieces = jnp.concatenate(_split3(lf), axis=0).astype(BF16)
        ri = lax.broadcasted_iota(jnp.int32, (tm, tm), 0)
        ci = lax.broadcasted_iota(jnp.int32, (tm, tm), 1)
        utri = jnp.where(ri <= ci, 1.0, 0.0).astype(BF16)
        cum3 = _dot(pieces, utri)
        tot3 = _dot(pieces, jnp.ones((tm, LANES), BF16))
        cum = cum3[0:8] + cum3[8:16] + cum3[16:24]
        tot = tot3[0:8] + tot3[8:16] + tot3[16:24]
        c = cum + _lane_tile(carry_ref[...], nrep)
        carry_ref[...] = carry_ref[...] + tot
        cpieces = jnp.concatenate(list(_split3(c * LOG2E)) + [jnp.zeros((LANES - 24, tm), F32)], axis=0)
        o["cp"][0] = cpieces.T.astype(BF16)
        glr = small[FOX_HEADS:FOX_HEADS + GLA_RANK].astype(BF16)
        pre = _dot(wa2t_ref[...], glr) + _lane_tile(bat_ref[...], nrep)
        o["laT"][0] = _log_sigmoid(pre) * (1.0 / GLA_TAU)

    groups = [(feat_dot(0, FOX_WIDTH), q_out), (feat_dot(FOX_WIDTH, 2 * FOX_WIDTH), k_out),
              (feat_dot(2 * FOX_WIDTH, 3 * FOX_WIDTH), v_out), (feat_dot(3 * FOX_WIDTH, 4 * FOX_WIDTH), gk_out),
              (feat_dot(_WF_SMALL, _WF_ROWS), small_out)]

    silu = lambda z: z * jax.nn.sigmoid(z)
    c0 = 0
    for name, width, fn in (("fgs", FOX_WIDTH, silu), ("gq", GLA_KW, lambda z: z * (GLA_DK ** -0.5)),
                            ("gv", GLA_VW, lambda z: z), ("ggs", GLA_VW, silu),
                            ("sa", D_MODEL, jax.nn.sigmoid), ("sb", D_MODEL, jax.nn.sigmoid)):
        for j in range(width // 512):
            def tok_dot(c=c0 + j * 512):
                return _dot(h_ref[...], wt_ref[:, c:c + 512])

            def tok_out(z, ref=o[name], j=j, fn=fn):
                ref[0, :, j * 512:(j + 1) * 512] = fn(z).astype(BF16)

            groups.append((tok_dot, tok_out))
        c0 += width

    pending = None
    for dot_fn, out_fn in groups:
        z = dot_fn()
        if pending is not None:
            pending[1](pending[0])
        pending = (z, out_fn)
    pending[1](pending[0])


def _proj_kernel(*refs, tm):
    in_refs = refs[:_N_PROJ_IN]
    out_refs = refs[_N_PROJ_IN:_N_PROJ_IN + len(_PROJ_OUTS)]
    carry_ref, h_ref = refs[_N_PROJ_IN + len(_PROJ_OUTS):]
    _proj_body(in_refs, out_refs, carry_ref, h_ref, pl.program_id(1) == 0, tm)


def _proj_out_specs(B, L, tm, imap_tok, imap_feat):
    tok = lambda w, dt: (jax.ShapeDtypeStruct((B, L, w), dt), pl.BlockSpec((1, tm, w), imap_tok))
    feat = lambda r, dt: (jax.ShapeDtypeStruct((B, r, L), dt), pl.BlockSpec((1, r, tm), imap_feat))
    outs = dict(
        qT=feat(FOX_WIDTH, BF16), kT=feat(FOX_WIDTH, F32), ktok=tok(FOX_WIDTH, BF16), cp=tok(LANES, BF16),
        vT=feat(FOX_WIDTH, F32), vTb=feat(FOX_HEADS * V_AUG, BF16), lfT=feat(FOX_HEADS, F32), fgs=tok(FOX_WIDTH, BF16),
        gq=tok(GLA_KW, BF16), gkT=feat(GLA_KW, BF16), gv=tok(GLA_VW, BF16),
        laT=feat(GLA_KW, F32), ggs=tok(GLA_VW, BF16), sa=tok(D_MODEL, BF16), sb=tok(D_MODEL, BF16))
    assert tuple(outs) == _PROJ_OUTS
    return [outs[n][0] for n in _PROJ_OUTS], [outs[n][1] for n in _PROJ_OUTS]


def _proj_in_specs(tm, imap_tok):
    D = D_MODEL
    return [pl.BlockSpec((1, tm, D), imap_tok),
            _const_spec((1, D)), _const_spec((_WF_ROWS, D)), _const_spec((D, _WT_COLS)),
            _const_spec((FOX_WIDTH, LANES)), _const_spec((FOX_WIDTH, LANES)),
            _const_spec((FOX_HEADS, LANES)), _const_spec((GLA_KW, GLA_RANK)),
            _const_spec((GLA_KW, LANES))]


def _proj_args(x, wts):
    return (x, wts["ln_g"], wts["wf"], wts["wt"], wts["qg"], wts["kg"], wts["fb"], wts["wa2t"], wts["bat"])


def _proj(x, wts, tm):
    B, L, D = x.shape
    assert L % tm == 0 and tm % LANES == 0
    out_shape, out_specs = _proj_out_specs(B, L, tm, lambda b, i: (b, i, 0), lambda b, i: (b, 0, i))
    res = pl.pallas_call(
        functools.partial(_proj_kernel, tm=tm),
        grid=(B, L // tm),
        in_specs=_proj_in_specs(tm, lambda b, i: (b, i, 0)),
        out_specs=out_specs,
        out_shape=out_shape,
        scratch_shapes=[pltpu.VMEM((FOX_HEADS, LANES), F32), pltpu.VMEM((tm, D), BF16)],
        compiler_params=pltpu.CompilerParams(
            dimension_semantics=("parallel", "arbitrary"), vmem_limit_bytes=VMEM_LIMIT),
        name="proj",
    )(*_proj_args(x, wts))
    return dict(zip(_PROJ_OUTS, res))


def _prep_weights(ln_g, w_in, fox_b_f, q_norm_g, k_norm_g, gla_w_a2, gla_b_a):
    w = w_in.T.astype(BF16)
    sl = lambda o, n: w[o:o + n]
    wf = jnp.concatenate([sl(_FQ, FOX_WIDTH), sl(_FK, FOX_WIDTH), sl(_FV, FOX_WIDTH), sl(_GK, GLA_KW),
                          sl(_FF, FOX_HEADS), sl(_GLR, GLA_RANK), jnp.zeros((8, D_MODEL), BF16)], axis=0)
    wt = jnp.concatenate([sl(_FG, FOX_WIDTH), sl(_GQ, GLA_KW), sl(_GV, GLA_VW), sl(_GG, GLA_VW),
                          sl(_MA, D_MODEL), sl(_MB, D_MODEL)], axis=0).T
    rep = lambda vec: jnp.broadcast_to(vec[:, None], (vec.shape[0], LANES)).astype(F32)
    return dict(ln_g=ln_g[None, :], wf=wf, wt=wt,
                qg=rep(jnp.tile(q_norm_g, FOX_HEADS)), kg=rep(jnp.tile(k_norm_g, FOX_HEADS)),
                fb=rep(fox_b_f), wa2t=gla_w_a2.T.astype(BF16), bat=rep(gla_b_a))


def _dec_substep(kbuf, vbuf, lbuf, qbd_ref, knT_ref, vn_ref, lfn_ref, fgs_ref, o_ref,
                 m_ref, l_ref, acc_ref, carry_ref, first, last, T):
    PP = DEC_PP
    H, P, HD = FOX_HEADS, PAGE_SIZE, FOX_HEAD_DIM
    R = H * T

    @pl.when(first)
    def _():
        m_ref[...] = jnp.full_like(m_ref, NEG_BIG)
        l_ref[...] = jnp.zeros_like(l_ref)
        acc_ref[...] = jnp.zeros_like(acc_ref)
        carry_ref[...] = jnp.zeros_like(carry_ref)

    lf = jnp.concatenate([lbuf[i] for i in range(PP)], axis=0) * LOG2E
    pieces = jnp.concatenate(_split3(lf), axis=0).astype(BF16)
    ri = lax.broadcasted_iota(jnp.int32, (P, 2 * P), 0)
    ci = lax.broadcasted_iota(jnp.int32, (P, 2 * P), 1)
    wcum = jnp.where((ri <= ci) | (ci >= P), 1.0, 0.0).astype(BF16)
    cw3 = _dot(pieces, wcum)
    n = PP * H
    cw = cw3[0:n] + cw3[n:2 * n] + cw3[2 * n:3 * n]
    off = carry_ref[...]
    bias = []
    for i in range(PP):
        c_i = cw[i * H:(i + 1) * H, 0:P] + off
        bias.append(jnp.broadcast_to(c_i[:, None, :], (H, T, P)).reshape(R, P))
        off = off + cw[i * H:(i + 1) * H, P:2 * P]
    carry_ref[...] = off

    def online(parts):
        m_prev = m_ref[:, 0:1]
        m_new = m_prev
        for s, _ in parts:
            m_new = jnp.maximum(m_new, jnp.max(s, axis=-1, keepdims=True))
        alpha = jnp.exp2(m_prev - m_new)
        l_new = alpha * l_ref[:, 0:1]
        acc = alpha * acc_ref[...]
        for s, v_dot in parts:
            p = jnp.exp2(s - m_new)
            l_new = l_new + jnp.sum(p, axis=-1, keepdims=True)
            acc = acc + v_dot(p.astype(BF16))
        acc_ref[...] = acc
        m_ref[...] = jnp.broadcast_to(m_new, (R, LANES))
        l_ref[...] = jnp.broadcast_to(l_new, (R, LANES))

    def pages(buf, lo, hi):
        return jnp.concatenate([buf[i].reshape(H * HD, P) for i in range(lo, hi)], axis=1).astype(BF16)

    qbd = qbd_ref[0]
    half = PP // 2
    parts = []
    for lo, hi in ((0, half), (half, PP)):
        s = _dot(qbd, pages(kbuf, lo, hi)) - jnp.concatenate(bias[lo:hi], axis=1)
        parts.append((s, lambda pb, lo=lo, hi=hi: _dot_nt(pb, pages(vbuf, lo, hi))))
    online(parts)

    @pl.when(last)
    def _():
        lfn = lfn_ref[0] * LOG2E
        pn = jnp.concatenate(_split3(lfn), axis=0).astype(BF16)
        r8 = lax.broadcasted_iota(jnp.int32, (T, T), 0)
        c8 = lax.broadcasted_iota(jnp.int32, (T, T), 1)
        u8 = jnp.where(r8 <= c8, 1.0, 0.0).astype(BF16)
        cn3 = _dot(pn, u8)
        c_new = carry_ref[:, 0:T] + cn3[0:H] + cn3[H:2 * H] + cn3[2 * H:3 * H]
        bias_n = jnp.broadcast_to(c_new[:, None, :], (H, T, T)).reshape(R, T)
        s_n = _dot(qbd, knT_ref[0].astype(BF16)) - bias_n
        t_row = lax.broadcasted_iota(jnp.int32, (R, T), 0) % T
        t_col = lax.broadcasted_iota(jnp.int32, (R, T), 1)
        s_n = jnp.where(t_col <= t_row, s_n, NEG_BIG)
        vn = vn_ref[0].astype(BF16)
        online([(s_n, lambda pb: _dot(pb, vn))])
        res = acc_ref[...] / l_ref[:, 0:1]
        lane_head = lax.broadcasted_iota(jnp.int32, (T, H * HD), 1) // HD
        out = jnp.zeros((T, H * HD), F32)
        for h in range(H):
            out = out + jnp.where(lane_head == h, res[h * T:(h + 1) * T], 0.0)
        o_ref[0] = out * fgs_ref[0]


def _fox_stages(qT_ref, ktok_ref, cp_ref, vTb_ref, fgs_ref, o_ref, wq_ref, m_ref, acc_ref,
                s0_ref, alpha3_ref, pb3_ref, qi, tq):
    def causal():
        kpos = lax.broadcasted_iota(jnp.int32, (tq, 2 * tq), 0)
        qpos = lax.broadcasted_iota(jnp.int32, (tq, 2 * tq), 1) % tq
        return kpos <= qpos

    def scores(p, k0):
        lo = p * LANES
        ka = jnp.concatenate([ktok_ref[0, pl.ds(k0, tq), lo:lo + LANES],
                              cp_ref[0, pl.ds(k0, tq), :]], axis=1)
        return _dot(ka, wq_ref[p])

    def colmax(p, s):
        m_prev = m_ref[p]
        m_new = jnp.maximum(m_prev, jnp.max(s, axis=0, keepdims=True))
        m_ref[p] = m_new
        return m_prev, m_new

    def expo(s, m_prev, m_new):
        return jnp.exp2(m_prev - m_new), jnp.exp2(s - m_new).astype(BF16)

    def weighted_v(p, k0, alpha, pb):
        for e in range(2):
            h = 2 * p + e
            va = vTb_ref[0, h * V_AUG:(h + 1) * V_AUG, pl.ds(k0, tq)]
            acc_ref[h] = alpha[:, e * tq:(e + 1) * tq] * acc_ref[h] + _dot(va, pb[:, e * tq:(e + 1) * tq])

    def trip(kj, last):
        k0 = pl.multiple_of(kj * tq, tq)
        k_prev = pl.multiple_of(jnp.maximum(kj - 1, 0) * tq, tq)
        if last:
            keep = causal()
            mask = lambda s: jnp.where(keep, s, NEG_BIG)
        else:
            mask = lambda s: s
        s0 = mask(s0_ref[...])
        mm0 = colmax(0, s0)
        s1 = mask(scores(1, k0))
        e0 = expo(s0, *mm0)
        weighted_v(3, k_prev, alpha3_ref[...], pb3_ref[...])
        mm1 = colmax(1, s1)
        s2 = mask(scores(2, k0))
        e1 = expo(s1, *mm1)
        weighted_v(0, k0, *e0)
        mm2 = colmax(2, s2)
        s3 = mask(scores(3, k0))
        e2 = expo(s2, *mm2)
        weighted_v(1, k0, *e1)
        mm3 = colmax(3, s3)
        if not last:
            s0_ref[...] = scores(0, pl.multiple_of((kj + 1) * tq, tq))
        e3 = expo(s3, *mm3)
        weighted_v(2, k0, *e2)
        alpha3_ref[...], pb3_ref[...] = e3

    def stage_a():
        rows = lax.broadcasted_iota(jnp.int32, (LANES, tq), 0)
        z64 = jnp.zeros((FOX_HEAD_DIM, tq), BF16)

        def bias_rows(h):
            hit = (rows == h) | (rows == h + 8) | (rows == h + 16)
            return jnp.where(hit, -1.0, 0.0).astype(BF16)

        for p in range(N_PAIRS):
            lo = p * LANES
            qe = qT_ref[0, lo:lo + FOX_HEAD_DIM, :]
            qo = qT_ref[0, lo + FOX_HEAD_DIM:lo + LANES, :]
            wq_ref[p] = jnp.concatenate(
                [jnp.concatenate([qe, z64, bias_rows(2 * p)], axis=0),
                 jnp.concatenate([z64, qo, bias_rows(2 * p + 1)], axis=0)], axis=1)
        m_ref[...] = jnp.full_like(m_ref, NEG_BIG)
        acc_ref[...] = jnp.zeros_like(acc_ref)
        s0_ref[...] = scores(0, 0)
        alpha3_ref[...] = jnp.ones_like(alpha3_ref)
        pb3_ref[...] = jnp.zeros_like(pb3_ref)

    def stage_b():
        def body(kj, carry):
            trip(kj, False)
            return carry

        lax.fori_loop(0, qi, body, 0)

    def stage_c():
        trip(qi, True)
        weighted_v(3, pl.multiple_of(qi * tq, tq), alpha3_ref[...], pb3_ref[...])

    def stage_d():
        outs = []
        for h in range(FOX_HEADS):
            a = acc_ref[h]
            outs.append(a[0:FOX_HEAD_DIM] * (1.0 / a[FOX_HEAD_DIM:FOX_HEAD_DIM + 1]))
        oT = jnp.concatenate(outs, axis=0)
        o_ref[0] = (oT.T * fgs_ref[0].astype(F32)).astype(BF16)

    return stage_a, stage_b, stage_c, stage_d


_N_FOX_IN = 5
_N_DEC_IN = 8


def _fox_dec_kernel(pt_ref, *refs, tq, blocks_per_seq, sub_per_step, sub_per_b, T):
    n_in = _N_FOX_IN + _N_DEC_IN
    fox_in = refs[:_N_FOX_IN]
    ck_hbm, cv_hbm, clf_hbm, qbd_ref, knT_ref, vn_ref, lfn_ref, fgs_ref = refs[_N_FOX_IN:n_in]
    o_ref, of_ref = refs[n_in:n_in + 2]
    (wq_ref, m_ref, acc_ref, s0_ref, alpha3_ref, pb3_ref,
     kbuf, vbuf, lbuf, sem, dm_ref, dl_ref, dacc_ref, dcarry_ref) = refs[n_in + 2:]
    g = pl.program_id(0)
    n_sub = pl.num_programs(0) * sub_per_step
    PP = DEC_PP

    def page_copies(s, slot):
        bd = s // sub_per_b
        j = s % sub_per_b
        out = []
        for i in range(PP):
            pg = pt_ref[bd, j * PP + i]
            out.append(pltpu.make_async_copy(ck_hbm.at[pg], kbuf.at[slot, i], sem.at[0, slot]))
            out.append(pltpu.make_async_copy(cv_hbm.at[pg], vbuf.at[slot, i], sem.at[1, slot]))
            out.append(pltpu.make_async_copy(clf_hbm.at[pg], lbuf.at[slot, i], sem.at[2, slot]))
        return out

    ahead = DEC_SLOTS - 1

    @pl.when(g == 0)
    def _():
        for a in range(ahead):
            for c in page_copies(a, a):
                c.start()

    stages = _fox_stages(*fox_in, o_ref, wq_ref, m_ref, acc_ref, s0_ref, alpha3_ref, pb3_ref,
                         g % blocks_per_seq, tq)
    assert sub_per_step == len(stages) and sub_per_step >= ahead and sub_per_step % DEC_SLOTS == 0
    for sub in range(sub_per_step):
        s = g * sub_per_step + sub
        slot = sub % DEC_SLOTS
        for c in page_copies(s, slot):
            c.wait()
        nxt = s + ahead
        nxt_slot = (sub + ahead) % DEC_SLOTS
        if sub + ahead < sub_per_step:
            for c in page_copies(nxt, nxt_slot):
                c.start()
        else:
            @pl.when(nxt < n_sub)
            def _():
                for c in page_copies(nxt, nxt_slot):
                    c.start()
        j = s % sub_per_b
        _dec_substep(kbuf.at[slot], vbuf.at[slot], lbuf.at[slot], qbd_ref, knT_ref, vn_ref, lfn_ref, fgs_ref,
                     of_ref, dm_ref, dl_ref, dacc_ref, dcarry_ref, j == 0, j == sub_per_b - 1, T)
        stages[sub]()


def _fox_dec(p, page_table, ck, cv, clf, qbd, knT, vn, lfn, fgs3):
    tq = FOX_TQ
    B, _, L = p["qT"].shape
    Bd, n_pages = page_table.shape
    T = vn.shape[1]
    H, HD, P, PP = FOX_HEADS, FOX_HEAD_DIM, PAGE_SIZE, DEC_PP
    R = H * T
    assert L % tq == 0 and n_pages % PP == 0
    nq = L // tq
    n_steps = B * nq
    sub_per_b = n_pages // PP
    n_sub = Bd * sub_per_b
    assert n_sub % n_steps == 0
    sub_per_step = n_sub // n_steps
    assert sub_per_b % sub_per_step == 0
    b_per = sub_per_b // sub_per_step

    per_seq = lambda s: pl.BlockSpec((1,) + s, lambda g, pt: (g // nq, 0, 0), pipeline_mode=pl.Buffered(1))
    per_b = lambda s: pl.BlockSpec((1,) + s, lambda g, pt: (g // b_per, 0, 0))
    hbm = pl.BlockSpec(memory_space=pl.ANY)
    grid_spec = pltpu.PrefetchScalarGridSpec(
        num_scalar_prefetch=1,
        grid=(n_steps,),
        in_specs=[pl.BlockSpec((1, FOX_WIDTH, tq), lambda g, pt: (g // nq, 0, g % nq)),
                  per_seq((L, FOX_WIDTH)), per_seq((L, LANES)), per_seq((FOX_HEADS * V_AUG, L)),
                  pl.BlockSpec((1, tq, FOX_WIDTH), lambda g, pt: (g // nq, g % nq, 0)),
                  hbm, hbm, hbm, per_b((R, FOX_WIDTH)), per_b((FOX_WIDTH, T)), per_b((T, FOX_WIDTH)),
                  per_b((H, T)), per_b((T, FOX_WIDTH))],
        out_specs=[pl.BlockSpec((1, tq, FOX_WIDTH), lambda g, pt: (g // nq, g % nq, 0)),
                   per_b((T, FOX_WIDTH))],
        scratch_shapes=[pltpu.VMEM((N_PAIRS, 2 * LANES, 2 * tq), BF16),
                        pltpu.VMEM((N_PAIRS, 1, 2 * tq), F32),
                        pltpu.VMEM((FOX_HEADS, V_AUG, tq), F32),
                        pltpu.VMEM((tq, 2 * tq), F32), pltpu.VMEM((1, 2 * tq), F32),
                        pltpu.VMEM((tq, 2 * tq), BF16),
                        pltpu.VMEM((DEC_SLOTS, PP, H, HD, P), F32), pltpu.VMEM((DEC_SLOTS, PP, H, HD, P), F32),
                        pltpu.VMEM((DEC_SLOTS, PP, H, P), F32), pltpu.SemaphoreType.DMA((3, DEC_SLOTS)),
                        pltpu.VMEM((R, LANES), F32), pltpu.VMEM((R, LANES), F32),
                        pltpu.VMEM((R, FOX_WIDTH), F32), pltpu.VMEM((H, LANES), F32)])
    return pl.pallas_call(
        functools.partial(_fox_dec_kernel, tq=tq, blocks_per_seq=nq, sub_per_step=sub_per_step,
                          sub_per_b=sub_per_b, T=T),
        grid_spec=grid_spec,
        out_shape=[jax.ShapeDtypeStruct((B, L, FOX_WIDTH), BF16),
                   jax.ShapeDtypeStruct((Bd, T, FOX_WIDTH), F32)],
        compiler_params=pltpu.CompilerParams(
            dimension_semantics=("arbitrary",), vmem_limit_bytes=VMEM_LIMIT_FUSED),
        name="fox_dec",
    )(page_table, p["qT"], p["ktok"], p["cp"], p["vTb"], p["fgs"], ck, cv, clf, qbd, knT, vn, lfn, fgs3)


GLA_MID = GLA_C // 2 - 1


def _gla_consts():
    i = jnp.arange(GLA_C)
    incl = (i[:, None] <= i[None, :]).astype(F32)
    upto_mid = (i[:, None] <= GLA_MID).astype(F32)
    uc = incl - upto_mid
    ud = 1.0 - incl
    ones = jnp.ones((GLA_C, LANES), F32)
    wfeat = jnp.concatenate([uc, ud, ones], axis=1)
    wfeat2 = jnp.concatenate([wfeat] * 2, axis=0).astype(BF16)
    ltok = jnp.concatenate([uc.T, incl.T], axis=0)
    ltok2 = jnp.concatenate([ltok] * 2, axis=1).astype(BF16)
    return wfeat2, ltok2


def _gla_kernel(gq_ref, gkT_ref, gv_ref, laT_ref, ggs_ref, gng_ref, wfeat_ref, ltok_ref,
                s0_ref, og_ref, s_ref):
    C = GLA_C

    @pl.when(pl.program_id(1) == 0)
    def _():
        s_ref[...] = s0_ref[...]

    ti = lax.broadcasted_iota(jnp.int32, (C, C), 0)
    si = lax.broadcasted_iota(jnp.int32, (C, C), 1)
    tril = si <= ti
    gng = gng_ref[...]
    for c in range(GLA_CHUNKS_PER_STEP):
        tsl = slice(c * C, (c + 1) * C)
        laT = laT_ref[0, :, tsl]
        g2 = jnp.concatenate(_split2(laT.T), axis=0).astype(BF16)
        btok = _dot(ltok_ref[...], g2)
        bc, b = btok[0:C], btok[C:2 * C]
        gT2 = jnp.concatenate(_split2(laT), axis=1).astype(BF16)
        bfeat = _dot(gT2, wfeat_ref[...])
        bcT, dT, blast = bfeat[:, 0:C], bfeat[:, C:2 * C], bfeat[:, 2 * C:]

        gq = gq_ref[0, tsl, :].astype(F32)
        q_in = (gq * jnp.exp(bc)).astype(BF16)
        q_s = (gq * jnp.exp(b)).astype(BF16)
        gkT = gkT_ref[0, :, tsl].astype(F32)
        k_inT = (gkT * jnp.exp(-bcT)).astype(BF16)
        k_decT = (gkT * jnp.exp(dT)).astype(BF16)
        decay = jnp.exp(blast)
        for h in range(GLA_HEADS):
            ks = slice(h * GLA_DK, (h + 1) * GLA_DK)
            vs = slice(h * GLA_DV, (h + 1) * GLA_DV)
            att = jnp.where(tril, _dot(q_in[:, ks], k_inT[ks, :]), 0.0).astype(BF16)
            vh = gv_ref[0, tsl, vs]
            s_h = s_ref[0, h]
            o = _dot(att, vh) + _dot(q_s[:, ks], s_h.astype(BF16))
            s_ref[0, h] = jnp.concatenate([decay[ks]] * 2, axis=1) * s_h + _dot(k_decT[ks, :], vh)
            ms = jnp.mean(o * o, axis=-1, keepdims=True)
            on = (o * lax.rsqrt(ms + EPS)) * gng
            og_ref[0, tsl, vs] = (on * ggs_ref[0, tsl, vs].astype(F32)).astype(BF16)


def _gla(p, s0, gla_norm_g):
    B, L, _ = p["gq"].shape
    C = GLA_C * GLA_CHUNKS_PER_STEP
    assert L % C == 0
    wfeat2, ltok2 = _gla_consts()
    tok = lambda w: pl.BlockSpec((1, C, w), lambda b, i: (b, i, 0))
    feat = lambda r: pl.BlockSpec((1, r, C), lambda b, i: (b, 0, i))
    st = pl.BlockSpec((1, GLA_HEADS, GLA_DK, GLA_DV), lambda b, i: (b, 0, 0, 0))
    return pl.pallas_call(
        _gla_kernel,
        grid=(B, L // C),
        in_specs=[tok(GLA_KW), feat(GLA_KW), tok(GLA_VW), feat(GLA_KW), tok(GLA_VW),
                  _const_spec((1, GLA_DV)), _const_spec(wfeat2.shape), _const_spec(ltok2.shape), st],
        out_specs=[tok(GLA_VW), st],
        out_shape=[jax.ShapeDtypeStruct((B, L, GLA_VW), BF16),
                   jax.ShapeDtypeStruct((B, GLA_HEADS, GLA_DK, GLA_DV), F32)],
        compiler_params=pltpu.CompilerParams(
            dimension_semantics=("parallel", "arbitrary"), vmem_limit_bytes=VMEM_LIMIT),
        name="gla",
    )(p["gq"], p["gkT"], p["gv"], p["laT"], p["ggs"], gla_norm_g[None, :], wfeat2, ltok2, s0)


def _merge_kernel(x_ref, of_ref, og_ref, sa_ref, sb_ref, wa_ref, wb_ref, wo_ref, y_ref):
    a = _dot(of_ref[0], wa_ref[...])
    b = _dot(og_ref[0], wb_ref[...])
    m = sa_ref[0].astype(F32) * a + sb_ref[0].astype(F32) * b
    y_ref[0] = x_ref[0] + _dot(m.astype(BF16), wo_ref[...])


def _merge(x, of, og, sa, sb, wm, tm):
    B, L, D = x.shape
    assert L % tm == 0
    tok = lambda w: pl.BlockSpec((1, tm, w), lambda b, i: (b, i, 0))
    return pl.pallas_call(
        _merge_kernel,
        grid=(B, L // tm),
        in_specs=[tok(D), tok(FOX_WIDTH), tok(GLA_VW), tok(D), tok(D),
                  _const_spec((FOX_WIDTH, D)), _const_spec((GLA_VW, D)), _const_spec((D, D))],
        out_specs=tok(D),
        out_shape=jax.ShapeDtypeStruct((B, L, D), F32),
        compiler_params=pltpu.CompilerParams(
            dimension_semantics=("parallel", "parallel"), vmem_limit_bytes=VMEM_LIMIT),
        name="merge",
    )(x, of, og, sa, sb, *wm)


def _gla_dec_kernel(gq_ref, gkT_ref, gv_ref, la_ref, laT_ref, ggs_ref, gng_ref, s0_ref, og_ref, s_ref, *, T):
    r8 = lax.broadcasted_iota(jnp.int32, (T, T), 0)
    c8 = lax.broadcasted_iota(jnp.int32, (T, T), 1)
    low = jnp.where(c8 <= r8, 1.0, 0.0).astype(BF16)
    ri = lax.broadcasted_iota(jnp.int32, (T, T + LANES), 0)
    ci = lax.broadcasted_iota(jnp.int32, (T, T + LANES), 1)
    wf = jnp.where((ri <= ci) | (ci >= T), 1.0, 0.0).astype(BF16)
    gng = gng_ref[...]
    for bb in range(GLA_DEC_BB):
        g2 = jnp.concatenate(_split2(la_ref[bb]), axis=0).astype(BF16)
        b = _dot(jnp.concatenate([low] * 2, axis=1), g2)
        gT2 = jnp.concatenate(_split2(laT_ref[bb]), axis=1).astype(BF16)
        bf = _dot(gT2, jnp.concatenate([wf] * 2, axis=0))
        bT, blast = bf[:, 0:T], bf[:, T:]
        q_in = (gq_ref[bb] * jnp.exp(b)).astype(BF16)
        gkT = gkT_ref[bb]
        k_inT = (gkT * jnp.exp(-bT)).astype(BF16)
        k_decT = (gkT * jnp.exp(blast[:, 0:T] - bT)).astype(BF16)
        decay = jnp.exp(blast)
        for h in range(GLA_HEADS):
            ks = slice(h * GLA_DK, (h + 1) * GLA_DK)
            vs = slice(h * GLA_DV, (h + 1) * GLA_DV)
            att = jnp.where(c8 <= r8, _dot(q_in[:, ks], k_inT[ks, :]), 0.0).astype(BF16)
            vh = gv_ref[bb, :, vs].astype(BF16)
            s_h = s0_ref[bb, h]
            o = _dot(att, vh) + _dot(q_in[:, ks], s_h.astype(BF16))
            s_ref[bb, h] = jnp.concatenate([decay[ks]] * 2, axis=1) * s_h + _dot(k_decT[ks, :], vh)
            ms = jnp.mean(o * o, axis=-1, keepdims=True)
            og_ref[bb, :, vs] = ((o * lax.rsqrt(ms + EPS)) * gng) * ggs_ref[bb, :, vs]


def _gla_dec(gq, gkT, gv, la, laT, ggs, gla_norm_g, s0):
    Bd, T, _ = gq.shape
    BB = GLA_DEC_BB
    assert Bd % BB == 0
    tok = lambda w: pl.BlockSpec((BB, T, w), lambda b: (b, 0, 0))
    feat = lambda r: pl.BlockSpec((BB, r, T), lambda b: (b, 0, 0))
    st = pl.BlockSpec((BB, GLA_HEADS, GLA_DK, GLA_DV), lambda b: (b, 0, 0, 0))
    return pl.pallas_call(
        functools.partial(_gla_dec_kernel, T=T),
        grid=(Bd // BB,),
        in_specs=[tok(GLA_KW), feat(GLA_KW), tok(GLA_VW), tok(GLA_KW), feat(GLA_KW), tok(GLA_VW),
                  _const_spec((1, GLA_DV)), st],
        out_specs=[tok(GLA_VW), st],
        out_shape=[jax.ShapeDtypeStruct((Bd, T, GLA_VW), F32),
                   jax.ShapeDtypeStruct((Bd, GLA_HEADS, GLA_DK, GLA_DV), F32)],
        compiler_params=pltpu.CompilerParams(
            dimension_semantics=("parallel",), vmem_limit_bytes=VMEM_LIMIT),
        name="gla_dec",
    )(gq, gkT, gv, la, laT, ggs, gla_norm_g[None, :], s0)


def _layer(x_prompt, x_sample, wts, wm, cache_k, cache_v, cache_logf, state, page_table, gla_norm_g):
    B, L, D = x_prompt.shape
    Bd, T, _ = x_sample.shape
    N = Bd * T
    H, HD = FOX_HEADS, FOX_HEAD_DIM

    ps = _proj(x_sample.reshape(1, N, D), wts, N)
    feat_bt = lambda a, r: a[0].astype(F32).reshape(r, Bd, T)
    tok_bt = lambda a: a[0].astype(F32).reshape(Bd, T, -1)
    qf = feat_bt(ps["qT"], FOX_WIDTH).reshape(H, HD, Bd, T)
    kf = feat_bt(ps["kT"], FOX_WIDTH).reshape(H, HD, Bd, T)
    vf = feat_bt(ps["vT"], FOX_WIDTH).reshape(H, HD, Bd, T)
    lff = feat_bt(ps["lfT"], H)
    q_bhtd = qf.transpose(2, 0, 3, 1)
    qbd = (q_bhtd[:, :, :, None, :] * jnp.eye(H, dtype=F32)[None, :, None, :, None]
           ).reshape(Bd, H * T, FOX_WIDTH).astype(BF16)
    ck = jnp.transpose(cache_k, (0, 2, 3, 1))
    cv = jnp.transpose(cache_v, (0, 2, 3, 1))
    clf = jnp.transpose(cache_logf, (0, 2, 1))

    pp = _proj(x_prompt, wts, PROJ_TM)
    of_p, of_s = _fox_dec(pp, page_table, ck, cv, clf, qbd,
                          kf.reshape(FOX_WIDTH, Bd, T).transpose(1, 0, 2),
                          vf.reshape(FOX_WIDTH, Bd, T).transpose(1, 2, 0),
                          lff.transpose(1, 0, 2), tok_bt(ps["fgs"]))
    og_p, s_p = _gla(pp, jnp.zeros((B, GLA_HEADS, GLA_DK, GLA_DV), F32), gla_norm_g)
    y_p = _merge(x_prompt, of_p, og_p, pp["sa"], pp["sb"], wm, MERGE_TM)
    k_p = pp["kT"].reshape(B, H, HD, L).transpose(0, 3, 1, 2)
    v_p = pp["vT"].reshape(B, H, HD, L).transpose(0, 3, 1, 2)
    lf_p = pp["lfT"].transpose(0, 2, 1)

    la_s = feat_bt(ps["laT"], GLA_KW)
    og_s, s_s = _gla_dec(tok_bt(ps["gq"]), feat_bt(ps["gkT"], GLA_KW).transpose(1, 0, 2), tok_bt(ps["gv"]),
                         la_s.transpose(1, 2, 0), la_s.transpose(1, 0, 2), tok_bt(ps["ggs"]),
                         gla_norm_g, state)
    y_s = _merge(x_sample.reshape(1, N, D), of_s.reshape(1, N, FOX_WIDTH).astype(BF16),
                 og_s.reshape(1, N, GLA_VW).astype(BF16), ps["sa"], ps["sb"], wm, N).reshape(Bd, T, D)
    k_s = kf.transpose(2, 3, 0, 1)
    v_s = vf.transpose(2, 3, 0, 1)
    lf_s = lff.transpose(1, 2, 0)
    return (y_p, y_s, k_p, v_p, lf_p, s_p, k_s, v_s, lf_s, s_s)


def kernel(x_prompt, x_sample, cache_k, cache_v, cache_logf, state_gla, page_table, ln_g, w_in, fox_b_f, q_norm_g, k_norm_g, gla_w_a2, gla_b_a, gla_norm_g, w_up_a, w_up_b, w_out):
    wts = _prep_weights(ln_g[0], w_in[0], fox_b_f[0], q_norm_g[0], k_norm_g[0], gla_w_a2[0], gla_b_a[0])
    wm = (w_up_a[0].astype(BF16), w_up_b[0].astype(BF16), w_out[0].astype(BF16))
    outs = _layer(x_prompt, x_sample, wts, wm, cache_k[0], cache_v[0], cache_logf[0], state_gla[0],
                  page_table, gla_norm_g[0])
    y_p, y_s = outs[0], outs[1]
    return (y_p, y_s) + tuple(o[None] for o in outs[2:])
```

```python
import functools

import jax
import jax.numpy as jnp
from jax import lax
from jax.experimental import pallas as pl
from jax.experimental.pallas import tpu as pltpu

F32 = jnp.float32
BF16 = jnp.bfloat16

D_MODEL = 1024
FOX_HEADS = 8
FOX_HEAD_DIM = 64
FOX_WIDTH = FOX_HEADS * FOX_HEAD_DIM
FOX_SCALE = FOX_HEAD_DIM ** -0.5
LOG2E = 1.4426950408889634
V_AUG = FOX_HEAD_DIM + 16
GLA_HEADS = 4
GLA_DK = 128
GLA_DV = 256
GLA_KW = GLA_HEADS * GLA_DK
GLA_VW = GLA_HEADS * GLA_DV
GLA_RANK = 16
GLA_TAU = 16.0
EPS = 1e-6
PAGE_SIZE = 128

LANES = 128
SUBLANES = 8
VMEM_LIMIT = 48 * 1024 * 1024
VMEM_LIMIT_FUSED = 56 * 1024 * 1024
PROJ_TM = 256
FOX_TQ = 256
GLA_C = 128
GLA_CHUNKS_PER_STEP = 4
GLA_DEC_BB = 8
DEC_PP = 16
DEC_SLOTS = 4

_SIZES = (FOX_WIDTH, FOX_WIDTH, FOX_WIDTH, FOX_HEADS, FOX_WIDTH,
          GLA_KW, GLA_KW, GLA_VW, GLA_RANK, GLA_VW, D_MODEL, D_MODEL)
_OFF = [0]
for _s in _SIZES:
    _OFF.append(_OFF[-1] + _s)
(_FQ, _FK, _FV, _FF, _FG, _GQ, _GK, _GV, _GLR, _GG, _MA, _MB) = _OFF[:-1]

_WF_SMALL = 4 * FOX_WIDTH
_WF_ROWS = _WF_SMALL + 32
_WT_COLS = FOX_WIDTH + GLA_KW + GLA_VW + GLA_VW + 2 * D_MODEL

NEG_BIG = -1e30
N_PAIRS = FOX_HEADS // 2


def _dot_nt(a, b):
    return lax.dot_general(a, b, (((1,), (1,)), ((), ())), preferred_element_type=F32)


def _dot(a, b):
    return jnp.dot(a, b, preferred_element_type=F32)


def _log_sigmoid(x):
    return -(jnp.maximum(-x, 0.0) + jnp.log1p(jnp.exp(-jnp.abs(x))))


def _split3(a):
    hi = a.astype(BF16).astype(F32)
    r = a - hi
    mid = r.astype(BF16).astype(F32)
    lo = (r - mid).astype(BF16).astype(F32)
    return hi, mid, lo


def _split2(a):
    hi = a.astype(BF16).astype(F32)
    return hi, (a - hi).astype(BF16).astype(F32)


def _lane_tile(a, n):
    return a if n == 1 else jnp.concatenate([a] * n, axis=1)


def _const_spec(shape):
    nd = len(shape)
    return pl.BlockSpec(shape, lambda *_: (0,) * nd, pipeline_mode=pl.Buffered(1))


_PROJ_OUTS = ("qT", "kT", "ktok", "cp", "vT", "vTb", "lfT", "fgs", "gq", "gkT", "gv", "laT", "ggs", "sa", "sb")
_N_PROJ_IN = 9


def _proj_body(in_refs, out_refs, carry_ref, h_ref, first_tile, tm):
    x_ref, lng_ref, wf_ref, wt_ref, qg_ref, kg_ref, fb_ref, wa2t_ref, bat_ref = in_refs
    o = dict(zip(_PROJ_OUTS, out_refs))
    nrep = tm // LANES

    @pl.when(first_tile)
    def _():
        carry_ref[...] = jnp.zeros_like(carry_ref)

    x = x_ref[0]
    ms = jnp.mean(x * x, axis=-1, keepdims=True)
    h_ref[...] = ((x * lax.rsqrt(ms + EPS)) * lng_ref[...]).astype(BF16)

    def headnorm(t, g_ref):
        outs = []
        for hh in range(FOX_HEADS):
            blk = t[hh * FOX_HEAD_DIM:(hh + 1) * FOX_HEAD_DIM]
            ssq = jnp.sum(blk * blk, axis=0, keepdims=True) * (1.0 / FOX_HEAD_DIM)
            g = _lane_tile(g_ref[hh * FOX_HEAD_DIM:(hh + 1) * FOX_HEAD_DIM], nrep)
            outs.append((blk * lax.rsqrt(ssq + EPS)) * g)
        return jnp.concatenate(outs, axis=0)

    def feat_dot(r0, r1):
        return lambda: _dot_nt(wf_ref[r0:r1], h_ref[...])

    def q_out(z):
        o["qT"][0] = (headnorm(z, qg_ref) * (FOX_SCALE * LOG2E)).astype(BF16)

    def k_out(z):
        k = headnorm(z, kg_ref)
        o["kT"][0] = k
        o["ktok"][0] = k.T.astype(BF16)

    def v_out(v):
        o["vT"][0] = v
        ones_rows = jnp.where(lax.broadcasted_iota(jnp.int32, (V_AUG - FOX_HEAD_DIM, tm), 0) == 0, 1.0, 0.0)
        vaug = []
        for hh in range(FOX_HEADS):
            vaug += [v[hh * FOX_HEAD_DIM:(hh + 1) * FOX_HEAD_DIM], ones_rows]
        o["vTb"][0] = jnp.concatenate(vaug, axis=0).astype(BF16)

    def gk_out(z):
        o["gkT"][0] = z.astype(BF16)

    def small_out(small):
        lf = _log_sigmoid(small[0:FOX_HEADS] + _lane_tile(fb_ref[...], nrep))
        o["lfT"][0] = lf
        pieces = jnp.concatenate(_split3(lf), axis=0).astype(BF16)
        ri = lax.broadcasted_iota(jnp.int32, (tm, tm), 0)
        ci = lax.broadcasted_iota(jnp.int32, (tm, tm), 1)
        utri = jnp.where(ri <= ci, 1.0, 0.0).astype(BF16)
        cum3 = _dot(pieces, utri)
        tot3 = _dot(pieces, jnp.ones((tm, LANES), BF16))
        cum = cum3[0:8] + cum3[8:16] + cum3[16:24]
        tot = tot3[0:8] + tot3[8:16] + tot3[16:24]
        c = cum + _lane_tile(carry_ref[...], nrep)
        carry_ref[...] = carry_ref[...] + tot
        cpieces = jnp.concatenate(list(_split3(c * LOG2E)) + [jnp.zeros((LANES - 24, tm), F32)], axis=0)
        o["cp"][0] = cpieces.T.astype(BF16)
        glr = small[FOX_HEADS:FOX_HEADS + GLA_RANK].astype(BF16)
        pre = _dot(wa2t_ref[...], glr) + _lane_tile(bat_ref[...], nrep)
        o["laT"][0] = _log_sigmoid(pre) * (1.0 / GLA_TAU)

    groups = [(feat_dot(0, FOX_WIDTH), q_out), (feat_dot(FOX_WIDTH, 2 * FOX_WIDTH), k_out),
              (feat_dot(2 * FOX_WIDTH, 3 * FOX_WIDTH), v_out), (feat_dot(3 * FOX_WIDTH, 4 * FOX_WIDTH), gk_out),
              (feat_dot(_WF_SMALL, _WF_ROWS), small_out)]

    silu = lambda z: z * jax.nn.sigmoid(z)
    c0 = 0
    for name, width, fn in (("fgs", FOX_WIDTH, silu), ("gq", GLA_KW, lambda z: z * (GLA_DK ** -0.5)),
                            ("gv", GLA_VW, lambda z: z), ("ggs", GLA_VW, silu),
                            ("sa", D_MODEL, jax.nn.sigmoid), ("sb", D_MODEL, jax.nn.sigmoid)):
        for j in range(width // 512):
            def tok_dot(c=c0 + j * 512):
                return _dot(h_ref[...], wt_ref[:, c:c + 512])

            def tok_out(z, ref=o[name], j=j, fn=fn):
                ref[0, :, j * 512:(j + 1) * 512] = fn(z).astype(BF16)

            groups.append((tok_dot, tok_out))
        c0 += width

    pending = None
    for dot_fn, out_fn in groups:
        z = dot_fn()
        if pending is not None:
            pending[1](pending[0])
        pending = (z, out_fn)
    pending[1](pending[0])


def _proj_kernel(*refs, tm):
    in_refs = refs[:_N_PROJ_IN]
    out_refs = refs[_N_PROJ_IN:_N_PROJ_IN + len(_PROJ_OUTS)]
    carry_ref, h_ref = refs[_N_PROJ_IN + len(_PROJ_OUTS):]
    _proj_body(in_refs, out_refs, carry_ref, h_ref, pl.program_id(1) == 0, tm)


def _proj_out_specs(B, L, tm, imap_tok, imap_feat):
    tok = lambda w, dt: (jax.ShapeDtypeStruct((B, L, w), dt), pl.BlockSpec((1, tm, w), imap_tok))
    feat = lambda r, dt: (jax.ShapeDtypeStruct((B, r, L), dt), pl.BlockSpec((1, r, tm), imap_feat))
    outs = dict(
        qT=feat(FOX_WIDTH, BF16), kT=feat(FOX_WIDTH, F32), ktok=tok(FOX_WIDTH, BF16), cp=tok(LANES, BF16),
        vT=feat(FOX_WIDTH, F32), vTb=feat(FOX_HEADS * V_AUG, BF16), lfT=feat(FOX_HEADS, F32), fgs=tok(FOX_WIDTH, BF16),
        gq=tok(GLA_KW, BF16), gkT=feat(GLA_KW, BF16), gv=tok(GLA_VW, BF16),
        laT=feat(GLA_KW, F32), ggs=tok(GLA_VW, BF16), sa=tok(D_MODEL, BF16), sb=tok(D_MODEL, BF16))
    assert tuple(outs) == _PROJ_OUTS
    return [outs[n][0] for n in _PROJ_OUTS], [outs[n][1] for n in _PROJ_OUTS]


def _proj_in_specs(tm, imap_tok):
    D = D_MODEL
    return [pl.BlockSpec((1, tm, D), imap_tok),
            _const_spec((1, D)), _const_spec((_WF_ROWS, D)), _const_spec((D, _WT_COLS)),
            _const_spec((FOX_WIDTH, LANES)), _const_spec((FOX_WIDTH, LANES)),
            _const_spec((FOX_HEADS, LANES)), _const_spec((GLA_KW, GLA_RANK)),
            _const_spec((GLA_KW, LANES))]


def _proj_args(x, wts):
    return (x, wts["ln_g"], wts["wf"], wts["wt"], wts["qg"], wts["kg"], wts["fb"], wts["wa2t"], wts["bat"])


def _proj(x, wts, tm):
    B, L, D = x.shape
    assert L % tm == 0 and tm % LANES == 0
    out_shape, out_specs = _proj_out_specs(B, L, tm, lambda b, i: (b, i, 0), lambda b, i: (b, 0, i))
    res = pl.pallas_call(
        functools.partial(_proj_kernel, tm=tm),
        grid=(B, L // tm),
        in_specs=_proj_in_specs(tm, lambda b, i: (b, i, 0)),
        out_specs=out_specs,
        out_shape=out_shape,
        scratch_shapes=[pltpu.VMEM((FOX_HEADS, LANES), F32), pltpu.VMEM((tm, D), BF16)],
        compiler_params=pltpu.CompilerParams(
            dimension_semantics=("parallel", "arbitrary"), vmem_limit_bytes=VMEM_LIMIT),
        name="proj",
    )(*_proj_args(x, wts))
    return dict(zip(_PROJ_OUTS, res))


def _prep_weights(ln_g, w_in, fox_b_f, q_norm_g, k_norm_g, gla_w_a2, gla_b_a):
    w = w_in.T.astype(BF16)
    sl = lambda o, n: w[o:o + n]
    wf = jnp.concatenate([sl(_FQ, FOX_WIDTH), sl(_FK, FOX_WIDTH), sl(_FV, FOX_WIDTH), sl(_GK, GLA_KW),
                          sl(_FF, FOX_HEADS), sl(_GLR, GLA_RANK), jnp.zeros((8, D_MODEL), BF16)], axis=0)
    wt = jnp.concatenate([sl(_FG, FOX_WIDTH), sl(_GQ, GLA_KW), sl(_GV, GLA_VW), sl(_GG, GLA_VW),
                          sl(_MA, D_MODEL), sl(_MB, D_MODEL)], axis=0).T
    rep = lambda vec: jnp.broadcast_to(vec[:, None], (vec.shape[0], LANES)).astype(F32)
    return dict(ln_g=ln_g[None, :], wf=wf, wt=wt,
                qg=rep(jnp.tile(q_norm_g, FOX_HEADS)), kg=rep(jnp.tile(k_norm_g, FOX_HEADS)),
                fb=rep(fox_b_f), wa2t=gla_w_a2.T.astype(BF16), bat=rep(gla_b_a))


def _dec_substep(kbuf, vbuf, lbuf, qbd_ref, knT_ref, vn_ref, lfn_ref, fgs_ref, o_ref,
                 m_ref, l_ref, acc_ref, carry_ref, first, last, T):
    PP = DEC_PP
    H, P, HD = FOX_HEADS, PAGE_SIZE, FOX_HEAD_DIM
    R = H * T

    @pl.when(first)
    def _():
        m_ref[...] = jnp.full_like(m_ref, NEG_BIG)
        l_ref[...] = jnp.zeros_like(l_ref)
        acc_ref[...] = jnp.zeros_like(acc_ref)
        carry_ref[...] = jnp.zeros_like(carry_ref)

    lf = jnp.concatenate([lbuf[i] for i in range(PP)], axis=0) * LOG2E
    pieces = jnp.concatenate(_split3(lf), axis=0).astype(BF16)
    ri = lax.broadcasted_iota(jnp.int32, (P, 2 * P), 0)
    ci = lax.broadcasted_iota(jnp.int32, (P, 2 * P), 1)
    wcum = jnp.where((ri <= ci) | (ci >= P), 1.0, 0.0).astype(BF16)
    cw3 = _dot(pieces, wcum)
    n = PP * H
    cw = cw3[0:n] + cw3[n:2 * n] + cw3[2 * n:3 * n]
    off = carry_ref[...]
    bias = []
    for i in range(PP):
        c_i = cw[i * H:(i + 1) * H, 0:P] + off
        bias.append(jnp.broadcast_to(c_i[:, None, :], (H, T, P)).reshape(R, P))
        off = off + cw[i * H:(i + 1) * H, P:2 * P]
    carry_ref[...] = off

    def online(parts):
        m_prev = m_ref[:, 0:1]
        m_new = m_prev
        for s, _ in parts:
            m_new = jnp.maximum(m_new, jnp.max(s, axis=-1, keepdims=True))
        alpha = jnp.exp2(m_prev - m_new)
        l_new = alpha * l_ref[:, 0:1]
        acc = alpha * acc_ref[...]
        for s, v_dot in parts:
            p = jnp.exp2(s - m_new)
            l_new = l_new + jnp.sum(p, axis=-1, keepdims=True)
            acc = acc + v_dot(p.astype(BF16))
        acc_ref[...] = acc
        m_ref[...] = jnp.broadcast_to(m_new, (R, LANES))
        l_ref[...] = jnp.broadcast_to(l_new, (R, LANES))

    def pages(buf, lo, hi):
        return jnp.concatenate([buf[i].reshape(H * HD, P) for i in range(lo, hi)], axis=1).astype(BF16)

    qbd = qbd_ref[0]
    half = PP // 2
    parts = []
    for lo, hi in ((0, half), (half, PP)):
        s = _dot(qbd, pages(kbuf, lo, hi)) - jnp.concatenate(bias[lo:hi], axis=1)
        parts.append((s, lambda pb, lo=lo, hi=hi: _dot_nt(pb, pages(vbuf, lo, hi))))
    online(parts)

    @pl.when(last)
    def _():
        lfn = lfn_ref[0] * LOG2E
        pn = jnp.concatenate(_split3(lfn), axis=0).astype(BF16)
        r8 = lax.broadcasted_iota(jnp.int32, (T, T), 0)
        c8 = lax.broadcasted_iota(jnp.int32, (T, T), 1)
        u8 = jnp.where(r8 <= c8, 1.0, 0.0).astype(BF16)
        cn3 = _dot(pn, u8)
        c_new = carry_ref[:, 0:T] + cn3[0:H] + cn3[H:2 * H] + cn3[2 * H:3 * H]
        bias_n = jnp.broadcast_to(c_new[:, None, :], (H, T, T)).reshape(R, T)
        s_n = _dot(qbd, knT_ref[0].astype(BF16)) - bias_n
        t_row = lax.broadcasted_iota(jnp.int32, (R, T), 0) % T
        t_col = lax.broadcasted_iota(jnp.int32, (R, T), 1)
        s_n = jnp.where(t_col <= t_row, s_n, NEG_BIG)
        vn = vn_ref[0].astype(BF16)
        online([(s_n, lambda pb: _dot(pb, vn))])
        res = acc_ref[...] / l_ref[:, 0:1]
        lane_head = lax.broadcasted_iota(jnp.int32, (T, H * HD), 1) // HD
        out = jnp.zeros((T, H * HD), F32)
        for h in range(H):
            out = out + jnp.where(lane_head == h, res[h * T:(h + 1) * T], 0.0)
        o_ref[0] = out * fgs_ref[0]


def _fox_stages(qT_ref, ktok_ref, cp_ref, vTb_ref, fgs_ref, o_ref, wq_ref, m_ref, acc_ref,
                s0_ref, alpha3_ref, pb3_ref, qi, tq):
    def causal():
        kpos = lax.broadcasted_iota(jnp.int32, (tq, 2 * tq), 0)
        qpos = lax.broadcasted_iota(jnp.int32, (tq, 2 * tq), 1) % tq
        return kpos <= qpos

    def scores(p, k0):
        lo = p * LANES
        ka = jnp.concatenate([ktok_ref[0, pl.ds(k0, tq), lo:lo + LANES],
                              cp_ref[0, pl.ds(k0, tq), :]], axis=1)
        return _dot(ka, wq_ref[p])

    def colmax(p, s):
        m_prev = m_ref[p]
        m_new = jnp.maximum(m_prev, jnp.max(s, axis=0, keepdims=True))
        m_ref[p] = m_new
        return m_prev, m_new

    def expo(s, m_prev, m_new):
        return jnp.exp2(m_prev - m_new), jnp.exp2(s - m_new).astype(BF16)

    def weighted_v(p, k0, alpha, pb):
        for e in range(2):
            h = 2 * p + e
            va = vTb_ref[0, h * V_AUG:(h + 1) * V_AUG, pl.ds(k0, tq)]
            acc_ref[h] = alpha[:, e * tq:(e + 1) * tq] * acc_ref[h] + _dot(va, pb[:, e * tq:(e + 1) * tq])

    def trip(kj, last):
        k0 = pl.multiple_of(kj * tq, tq)
        k_prev = pl.multiple_of(jnp.maximum(kj - 1, 0) * tq, tq)
        if last:
            keep = causal()
            mask = lambda s: jnp.where(keep, s, NEG_BIG)
        else:
            mask = lambda s: s
        s0 = mask(s0_ref[...])
        mm0 = colmax(0, s0)
        s1 = mask(scores(1, k0))
        e0 = expo(s0, *mm0)
        weighted_v(3, k_prev, alpha3_ref[...], pb3_ref[...])
        mm1 = colmax(1, s1)
        s2 = mask(scores(2, k0))
        e1 = expo(s1, *mm1)
        weighted_v(0, k0, *e0)
        mm2 = colmax(2, s2)
        s3 = mask(scores(3, k0))
        e2 = expo(s2, *mm2)
        weighted_v(1, k0, *e1)
        mm3 = colmax(3, s3)
        if not last:
            s0_ref[...] = scores(0, pl.multiple_of((kj + 1) * tq, tq))
        e3 = expo(s3, *mm3)
        weighted_v(2, k0, *e2)
        alpha3_ref[...], pb3_ref[...] = e3

    def stage_a():
        rows = lax.broadcasted_iota(jnp.int32, (LANES, tq), 0)
        z64 = jnp.zeros((FOX_HEAD_DIM, tq), BF16)

        def bias_rows(h):
            hit = (rows == h) | (rows == h + 8) | (rows == h + 16)
            return jnp.where(hit, -1.0, 0.0).astype(BF16)

        for p in range(N_PAIRS):
            lo = p * LANES
            qe = qT_ref[0, lo:lo + FOX_HEAD_DIM, :]
            qo = qT_ref[0, lo + FOX_HEAD_DIM:lo + LANES, :]
            wq_ref[p] = jnp.concatenate(
                [jnp.concatenate([qe, z64, bias_rows(2 * p)], axis=0),
                 jnp.concatenate([z64, qo, bias_rows(2 * p + 1)], axis=0)], axis=1)
        m_ref[...] = jnp.full_like(m_ref, NEG_BIG)
        acc_ref[...] = jnp.zeros_like(acc_ref)
        s0_ref[...] = scores(0, 0)
        alpha3_ref[...] = jnp.ones_like(alpha3_ref)
        pb3_ref[...] = jnp.zeros_like(pb3_ref)

    def stage_b():
        def body(kj, carry):
            trip(kj, False)
            return carry

        lax.fori_loop(0, qi, body, 0)

    def stage_c():
        trip(qi, True)
        weighted_v(3, pl.multiple_of(qi * tq, tq), alpha3_ref[...], pb3_ref[...])

    def stage_d():
        outs = []
        for h in range(FOX_HEADS):
            a = acc_ref[h]
            outs.append(a[0:FOX_HEAD_DIM] * (1.0 / a[FOX_HEAD_DIM:FOX_HEAD_DIM + 1]))
        oT = jnp.concatenate(outs, axis=0)
        o_ref[0] = (oT.T * fgs_ref[0].astype(F32)).astype(BF16)

    return stage_a, stage_b, stage_c, stage_d


_N_FOX_IN = 5
_N_DEC_IN = 8


def _fox_dec_kernel(pt_ref, *refs, tq, blocks_per_seq, sub_per_step, sub_per_b, T):
    n_in = _N_FOX_IN + _N_DEC_IN
    fox_in = refs[:_N_FOX_IN]
    ck_hbm, cv_hbm, clf_hbm, qbd_ref, knT_ref, vn_ref, lfn_ref, fgs_ref = refs[_N_FOX_IN:n_in]
    o_ref, of_ref = refs[n_in:n_in + 2]
    (wq_ref, m_ref, acc_ref, s0_ref, alpha3_ref, pb3_ref,
     kbuf, vbuf, lbuf, sem, dm_ref, dl_ref, dacc_ref, dcarry_ref) = refs[n_in + 2:]
    g = pl.program_id(0)
    n_sub = pl.num_programs(0) * sub_per_step
    PP = DEC_PP

    def page_copies(s, slot):
        bd = s // sub_per_b
        j = s % sub_per_b
        out = []
        for i in range(PP):
            pg = pt_ref[bd, j * PP + i]
            out.append(pltpu.make_async_copy(ck_hbm.at[pg], kbuf.at[slot, i], sem.at[0, slot]))
            out.append(pltpu.make_async_copy(cv_hbm.at[pg], vbuf.at[slot, i], sem.at[1, slot]))
            out.append(pltpu.make_async_copy(clf_hbm.at[pg], lbuf.at[slot, i], sem.at[2, slot]))
        return out

    ahead = DEC_SLOTS - 1

    @pl.when(g == 0)
    def _():
        for a in range(ahead):
            for c in page_copies(a, a):
                c.start()

    stages = _fox_stages(*fox_in, o_ref, wq_ref, m_ref, acc_ref, s0_ref, alpha3_ref, pb3_ref,
                         g % blocks_per_seq, tq)
    assert sub_per_step == len(stages) and sub_per_step >= ahead and sub_per_step % DEC_SLOTS == 0
    for sub in range(sub_per_step):
        s = g * sub_per_step + sub
        slot = sub % DEC_SLOTS
        for c in page_copies(s, slot):
            c.wait()
        nxt = s + ahead
        nxt_slot = (sub + ahead) % DEC_SLOTS
        if sub + ahead < sub_per_step:
            for c in page_copies(nxt, nxt_slot):
                c.start()
        else:
            @pl.when(nxt < n_sub)
            def _():
                for c in page_copies(nxt, nxt_slot):
                    c.start()
        j = s % sub_per_b
        _dec_substep(kbuf.at[slot], vbuf.at[slot], lbuf.at[slot], qbd_ref, knT_ref, vn_ref, lfn_ref, fgs_ref,
                     of_ref, dm_ref, dl_ref, dacc_ref, dcarry_ref, j == 0, j == sub_per_b - 1, T)
        stages[sub]()


def _fox_dec(p, page_table, ck, cv, clf, qbd, knT, vn, lfn, fgs3):
    tq = FOX_TQ
    B, _, L = p["qT"].shape
    Bd, n_pages = page_table.shape
    T = vn.shape[1]
    H, HD, P, PP = FOX_HEADS, FOX_HEAD_DIM, PAGE_SIZE, DEC_PP
    R = H * T
    assert L % tq == 0 and n_pages % PP == 0
    nq = L // tq
    n_steps = B * nq
    sub_per_b = n_pages // PP
    n_sub = Bd * sub_per_b
    assert n_sub % n_steps == 0
    sub_per_step = n_sub // n_steps
    assert sub_per_b % sub_per_step == 0
    b_per = sub_per_b // sub_per_step

    per_seq = lambda s: pl.BlockSpec((1,) + s, lambda g, pt: (g // nq, 0, 0), pipeline_mode=pl.Buffered(1))
    per_b = lambda s: pl.BlockSpec((1,) + s, lambda g, pt: (g // b_per, 0, 0))
    hbm = pl.BlockSpec(memory_space=pl.ANY)
    grid_spec = pltpu.PrefetchScalarGridSpec(
        num_scalar_prefetch=1,
        grid=(n_steps,),
        in_specs=[pl.BlockSpec((1, FOX_WIDTH, tq), lambda g, pt: (g // nq, 0, g % nq)),
                  per_seq((L, FOX_WIDTH)), per_seq((L, LANES)), per_seq((FOX_HEADS * V_AUG, L)),
                  pl.BlockSpec((1, tq, FOX_WIDTH), lambda g, pt: (g // nq, g % nq, 0)),
                  hbm, hbm, hbm, per_b((R, FOX_WIDTH)), per_b((FOX_WIDTH, T)), per_b((T, FOX_WIDTH)),
                  per_b((H, T)), per_b((T, FOX_WIDTH))],
        out_specs=[pl.BlockSpec((1, tq, FOX_WIDTH), lambda g, pt: (g // nq, g % nq, 0)),
                   per_b((T, FOX_WIDTH))],
        scratch_shapes=[pltpu.VMEM((N_PAIRS, 2 * LANES, 2 * tq), BF16),
                        pltpu.VMEM((N_PAIRS, 1, 2 * tq), F32),
                        pltpu.VMEM((FOX_HEADS, V_AUG, tq), F32),
                        pltpu.VMEM((tq, 2 * tq), F32), pltpu.VMEM((1, 2 * tq), F32),
                        pltpu.VMEM((tq, 2 * tq), BF16),
                        pltpu.VMEM((DEC_SLOTS, PP, H, HD, P), F32), pltpu.VMEM((DEC_SLOTS, PP, H, HD, P), F32),
                        pltpu.VMEM((DEC_SLOTS, PP, H, P), F32), pltpu.SemaphoreType.DMA((3, DEC_SLOTS)),
                        pltpu.VMEM((R, LANES), F32), pltpu.VMEM((R, LANES), F32),
                        pltpu.VMEM((R, FOX_WIDTH), F32), pltpu.VMEM((H, LANES), F32)])
    return pl.pallas_call(
        functools.partial(_fox_dec_kernel, tq=tq, blocks_per_seq=nq, sub_per_step=sub_per_step,
                          sub_per_b=sub_per_b, T=T),
        grid_spec=grid_spec,
        out_shape=[jax.ShapeDtypeStruct((B, L, FOX_WIDTH), BF16),
                   jax.ShapeDtypeStruct((Bd, T, FOX_WIDTH), F32)],
        compiler_params=pltpu.CompilerParams(
            dimension_semantics=("arbitrary",), vmem_limit_bytes=VMEM_LIMIT_FUSED),
        name="fox_dec",
    )(page_table, p["qT"], p["ktok"], p["cp"], p["vTb"], p["fgs"], ck, cv, clf, qbd, knT, vn, lfn, fgs3)


GLA_MID = GLA_C // 2 - 1


def _gla_consts():
    i = jnp.arange(GLA_C)
    incl = (i[:, None] <= i[None, :]).astype(F32)
    upto_mid = (i[:, None] <= GLA_MID).astype(F32)
    uc = incl - upto_mid
    ud = 1.0 - incl
    ones = jnp.ones((GLA_C, LANES), F32)
    wfeat = jnp.concatenate([uc, ud, ones], axis=1)
    wfeat2 = jnp.concatenate([wfeat] * 2, axis=0).astype(BF16)
    ltok = jnp.concatenate([uc.T, incl.T], axis=0)
    ltok2 = jnp.concatenate([ltok] * 2, axis=1).astype(BF16)
    return wfeat2, ltok2


def _gla_kernel(gq_ref, gkT_ref, gv_ref, laT_ref, ggs_ref, gng_ref, wfeat_ref, ltok_ref, s0_ref,
                x_ref, of_ref, sa_ref, sb_ref, wa_ref, wb_ref, wo_ref, y_ref, s_ref, og_ref):
    C = GLA_C

    @pl.when(pl.program_id(1) == 0)
    def _():
        s_ref[...] = s0_ref[...]

    def up_proj(r):
        return _dot(of_ref[0, r, :], wa_ref[...]), _dot(og_ref[r, :], wb_ref[...])

    def mix_out(r, a, b):
        m = sa_ref[0, r, :].astype(F32) * a + sb_ref[0, r, :].astype(F32) * b
        y_ref[0, r, :] = x_ref[0, r, :] + _dot(m.astype(BF16), wo_ref[...])

    ti = lax.broadcasted_iota(jnp.int32, (C, C), 0)
    si = lax.broadcasted_iota(jnp.int32, (C, C), 1)
    tril = si <= ti
    gng = gng_ref[...]
    for c in range(GLA_CHUNKS_PER_STEP):
        tsl = slice(c * C, (c + 1) * C)
        laT = laT_ref[0, :, tsl]
        g2 = jnp.concatenate(_split2(laT.T), axis=0).astype(BF16)
        btok = _dot(ltok_ref[...], g2)
        bc, b = btok[0:C], btok[C:2 * C]
        gT2 = jnp.concatenate(_split2(laT), axis=1).astype(BF16)
        bfeat = _dot(gT2, wfeat_ref[...])
        bcT, dT, blast = bfeat[:, 0:C], bfeat[:, C:2 * C], bfeat[:, 2 * C:]

        gq = gq_ref[0, tsl, :].astype(F32)
        q_in = (gq * jnp.exp(bc)).astype(BF16)
        q_s = (gq * jnp.exp(b)).astype(BF16)
        gkT = gkT_ref[0, :, tsl].astype(F32)
        k_inT = (gkT * jnp.exp(-bcT)).astype(BF16)
        k_decT = (gkT * jnp.exp(dT)).astype(BF16)
        decay = jnp.exp(blast)
        prev = slice((c - 1) * C, c * C) if c > 0 else None
        ab = up_proj(prev) if prev is not None else None
        for h in range(GLA_HEADS):
            ks = slice(h * GLA_DK, (h + 1) * GLA_DK)
            vs = slice(h * GLA_DV, (h + 1) * GLA_DV)
            att = jnp.where(tril, _dot(q_in[:, ks], k_inT[ks, :]), 0.0).astype(BF16)
            vh = gv_ref[0, tsl, vs]
            s_h = s_ref[0, h]
            o = _dot(att, vh) + _dot(q_s[:, ks], s_h.astype(BF16))
            s_ref[0, h] = jnp.concatenate([decay[ks]] * 2, axis=1) * s_h + _dot(k_decT[ks, :], vh)
            ms = jnp.mean(o * o, axis=-1, keepdims=True)
            on = (o * lax.rsqrt(ms + EPS)) * gng
            og_ref[tsl, vs] = (on * ggs_ref[0, tsl, vs].astype(F32)).astype(BF16)
            if h == GLA_HEADS // 2 - 1 and prev is not None:
                mix_out(prev, *ab)
    last = slice((GLA_CHUNKS_PER_STEP - 1) * C, GLA_CHUNKS_PER_STEP * C)
    mix_out(last, *up_proj(last))


def _gla_merge(p, s0, gla_norm_g, x, of, wm):
    B, L, D = x.shape
    C = GLA_C * GLA_CHUNKS_PER_STEP
    assert L % C == 0
    wfeat2, ltok2 = _gla_consts()
    tok = lambda w: pl.BlockSpec((1, C, w), lambda b, i: (b, i, 0))
    feat = lambda r: pl.BlockSpec((1, r, C), lambda b, i: (b, 0, i))
    st = pl.BlockSpec((1, GLA_HEADS, GLA_DK, GLA_DV), lambda b, i: (b, 0, 0, 0))
    return pl.pallas_call(
        _gla_kernel,
        grid=(B, L // C),
        in_specs=[tok(GLA_KW), feat(GLA_KW), tok(GLA_VW), feat(GLA_KW), tok(GLA_VW),
                  _const_spec((1, GLA_DV)), _const_spec(wfeat2.shape), _const_spec(ltok2.shape), st,
                  tok(D), tok(FOX_WIDTH), tok(D), tok(D),
                  _const_spec((FOX_WIDTH, D)), _const_spec((GLA_VW, D)), _const_spec((D, D))],
        out_specs=[tok(D), st],
        out_shape=[jax.ShapeDtypeStruct((B, L, D), F32),
                   jax.ShapeDtypeStruct((B, GLA_HEADS, GLA_DK, GLA_DV), F32)],
        scratch_shapes=[pltpu.VMEM((C, GLA_VW), BF16)],
        compiler_params=pltpu.CompilerParams(
            dimension_semantics=("parallel", "arbitrary"), vmem_limit_bytes=VMEM_LIMIT),
        name="gla_merge",
    )(p["gq"], p["gkT"], p["gv"], p["laT"], p["ggs"], gla_norm_g[None, :], wfeat2, ltok2, s0,
      x, of, p["sa"], p["sb"], *wm)


def _merge_kernel(x_ref, of_ref, og_ref, sa_ref, sb_ref, wa_ref, wb_ref, wo_ref, y_ref):
    a = _dot(of_ref[0], wa_ref[...])
    b = _dot(og_ref[0], wb_ref[...])
    m = sa_ref[0].astype(F32) * a + sb_ref[0].astype(F32) * b
    y_ref[0] = x_ref[0] + _dot(m.astype(BF16), wo_ref[...])


def _merge(x, of, og, sa, sb, wm, tm):
    B, L, D = x.shape
    assert L % tm == 0
    tok = lambda w: pl.BlockSpec((1, tm, w), lambda b, i: (b, i, 0))
    return pl.pallas_call(
        _merge_kernel,
        grid=(B, L // tm),
        in_specs=[tok(D), tok(FOX_WIDTH), tok(GLA_VW), tok(D), tok(D),
                  _const_spec((FOX_WIDTH, D)), _const_spec((GLA_VW, D)), _const_spec((D, D))],
        out_specs=tok(D),
        out_shape=jax.ShapeDtypeStruct((B, L, D), F32),
        compiler_params=pltpu.CompilerParams(
            dimension_semantics=("parallel", "parallel"), vmem_limit_bytes=VMEM_LIMIT),
        name="merge",
    )(x, of, og, sa, sb, *wm)


def _gla_dec_kernel(gq_ref, gkT_ref, gv_ref, la_ref, laT_ref, ggs_ref, gng_ref, s0_ref, og_ref, s_ref, *, T):
    r8 = lax.broadcasted_iota(jnp.int32, (T, T), 0)
    c8 = lax.broadcasted_iota(jnp.int32, (T, T), 1)
    low = jnp.where(c8 <= r8, 1.0, 0.0).astype(BF16)
    ri = lax.broadcasted_iota(jnp.int32, (T, T + LANES), 0)
    ci = lax.broadcasted_iota(jnp.int32, (T, T + LANES), 1)
    wf = jnp.where((ri <= ci) | (ci >= T), 1.0, 0.0).astype(BF16)
    gng = gng_ref[...]
    for bb in range(GLA_DEC_BB):
        g2 = jnp.concatenate(_split2(la_ref[bb]), axis=0).astype(BF16)
        b = _dot(jnp.concatenate([low] * 2, axis=1), g2)
        gT2 = jnp.concatenate(_split2(laT_ref[bb]), axis=1).astype(BF16)
        bf = _dot(gT2, jnp.concatenate([wf] * 2, axis=0))
        bT, blast = bf[:, 0:T], bf[:, T:]
        q_in = (gq_ref[bb] * jnp.exp(b)).astype(BF16)
        gkT = gkT_ref[bb]
        k_inT = (gkT * jnp.exp(-bT)).astype(BF16)
        k_decT = (gkT * jnp.exp(blast[:, 0:T] - bT)).astype(BF16)
        decay = jnp.exp(blast)
        for h in range(GLA_HEADS):
            ks = slice(h * GLA_DK, (h + 1) * GLA_DK)
            vs = slice(h * GLA_DV, (h + 1) * GLA_DV)
            att = jnp.where(c8 <= r8, _dot(q_in[:, ks], k_inT[ks, :]), 0.0).astype(BF16)
            vh = gv_ref[bb, :, vs].astype(BF16)
            s_h = s0_ref[bb, h]
            o = _dot(att, vh) + _dot(q_in[:, ks], s_h.astype(BF16))
            s_ref[bb, h] = jnp.concatenate([decay[ks]] * 2, axis=1) * s_h + _dot(k_decT[ks, :], vh)
            ms = jnp.mean(o * o, axis=-1, keepdims=True)
            og_ref[bb, :, vs] = ((o * lax.rsqrt(ms + EPS)) * gng) * ggs_ref[bb, :, vs]


def _gla_dec(gq, gkT, gv, la, laT, ggs, gla_norm_g, s0):
    Bd, T, _ = gq.shape
    BB = GLA_DEC_BB
    assert Bd % BB == 0
    tok = lambda w: pl.BlockSpec((BB, T, w), lambda b: (b, 0, 0))
    feat = lambda r: pl.BlockSpec((BB, r, T), lambda b: (b, 0, 0))
    st = pl.BlockSpec((BB, GLA_HEADS, GLA_DK, GLA_DV), lambda b: (b, 0, 0, 0))
    return pl.pallas_call(
        functools.partial(_gla_dec_kernel, T=T),
        grid=(Bd // BB,),
        in_specs=[tok(GLA_KW), feat(GLA_KW), tok(GLA_VW), tok(GLA_KW), feat(GLA_KW), tok(GLA_VW),
                  _const_spec((1, GLA_DV)), st],
        out_specs=[tok(GLA_VW), st],
        out_shape=[jax.ShapeDtypeStruct((Bd, T, GLA_VW), F32),
                   jax.ShapeDtypeStruct((Bd, GLA_HEADS, GLA_DK, GLA_DV), F32)],
        compiler_params=pltpu.CompilerParams(
            dimension_semantics=("parallel",), vmem_limit_bytes=VMEM_LIMIT),
        name="gla_dec",
    )(gq, gkT, gv, la, laT, ggs, gla_norm_g[None, :], s0)


def _layer(x_prompt, x_sample, wts, wm, cache_k, cache_v, cache_logf, state, page_table, gla_norm_g):
    B, L, D = x_prompt.shape
    Bd, T, _ = x_sample.shape
    N = Bd * T
    H, HD = FOX_HEADS, FOX_HEAD_DIM

    ps = _proj(x_sample.reshape(1, N, D), wts, N)
    feat_bt = lambda a, r: a[0].astype(F32).reshape(r, Bd, T)
    tok_bt = lambda a: a[0].astype(F32).reshape(Bd, T, -1)
    qf = feat_bt(ps["qT"], FOX_WIDTH).reshape(H, HD, Bd, T)
    kf = feat_bt(ps["kT"], FOX_WIDTH).reshape(H, HD, Bd, T)
    vf = feat_bt(ps["vT"], FOX_WIDTH).reshape(H, HD, Bd, T)
    lff = feat_bt(ps["lfT"], H)
    q_bhtd = qf.transpose(2, 0, 3, 1)
    qbd = (q_bhtd[:, :, :, None, :] * jnp.eye(H, dtype=F32)[None, :, None, :, None]
           ).reshape(Bd, H * T, FOX_WIDTH).astype(BF16)
    ck = jnp.transpose(cache_k, (0, 2, 3, 1))
    cv = jnp.transpose(cache_v, (0, 2, 3, 1))
    clf = jnp.transpose(cache_logf, (0, 2, 1))

    pp = _proj(x_prompt, wts, PROJ_TM)
    of_p, of_s = _fox_dec(pp, page_table, ck, cv, clf, qbd,
                          kf.reshape(FOX_WIDTH, Bd, T).transpose(1, 0, 2),
                          vf.reshape(FOX_WIDTH, Bd, T).transpose(1, 2, 0),
                          lff.transpose(1, 0, 2), tok_bt(ps["fgs"]))
    y_p, s_p = _gla_merge(pp, jnp.zeros((B, GLA_HEADS, GLA_DK, GLA_DV), F32), gla_norm_g, x_prompt, of_p, wm)
    k_p = pp["kT"].reshape(B, H, HD, L).transpose(0, 3, 1, 2)
    v_p = pp["vT"].reshape(B, H, HD, L).transpose(0, 3, 1, 2)
    lf_p = pp["lfT"].transpose(0, 2, 1)

    la_s = feat_bt(ps["laT"], GLA_KW)
    og_s, s_s = _gla_dec(tok_bt(ps["gq"]), feat_bt(ps["gkT"], GLA_KW).transpose(1, 0, 2), tok_bt(ps["gv"]),
                         la_s.transpose(1, 2, 0), la_s.transpose(1, 0, 2), tok_bt(ps["ggs"]),
                         gla_norm_g, state)
    y_s = _merge(x_sample.reshape(1, N, D), of_s.reshape(1, N, FOX_WIDTH).astype(BF16),
                 og_s.reshape(1, N, GLA_VW).astype(BF16), ps["sa"], ps["sb"], wm, N).reshape(Bd, T, D)
    k_s = kf.transpose(2, 3, 0, 1)
    v_s = vf.transpose(2, 3, 0, 1)
    lf_s = lff.transpose(1, 2, 0)
    return (y_p, y_s, k_p, v_p, lf_p, s_p, k_s, v_s, lf_s, s_s)


def kernel(x_prompt, x_sample, cache_k, cache_v, cache_logf, state_gla, page_table, ln_g, w_in, fox_b_f, q_norm_g, k_norm_g, gla_w_a2, gla_b_a, gla_norm_g, w_up_a, w_up_b, w_out):
    wts = _prep_weights(ln_g[0], w_in[0], fox_b_f[0], q_norm_g[0], k_norm_g[0], gla_w_a2[0], gla_b_a[0])
    wm = (w_up_a[0].astype(BF16), w_up_b[0].astype(BF16), w_out[0].astype(BF16))
    outs = _layer(x_prompt, x_sample, wts, wm, cache_k[0], cache_v[0], cache_logf[0], state_gla[0],
                  page_table, gla_norm_g[0])
    y_p, y_s = outs[0], outs[1]
    return (y_p, y_s) + tuple(o[None] for o in outs[2:])
```
